```python
import math
import jax
import jax.numpy as jnp
from jax import lax
import numpy as np

D_MODEL = 1024
BATCH = 8
SEQ = 2048
DEPTH = 1
DEC_BATCH = 128
DEC_SEQ = 1
PAST_LEN = 2048
PAGE_SIZE = 128

SB_HEADS = 8
SB_HEAD_DIM = 64
SB_BLOCK = 128
SB_BIAS_INIT = -7.0
GDN_HEADS = 8
GDN_K_DIM = 64
GDN_V_DIM = 64
GDN_CONV = 4
GDN_CHUNK = 64
N_MEM = 256
MEM_HEADS = 4
MEM_HEAD_DIM = 128
N_EXPERTS = 64
TOP_K = 8
N_GROUPS = 8
TOPK_GROUPS = 4
D_EXPERT = 256
D_SHARED = 256
ROUTE_SCALE = 2.5
MOE_BLOCK = 64
DN_ALPHA = (2 * DEPTH) ** 0.25
DN_BETA = (8 * DEPTH) ** -0.25
LN_EPS = 1e-5
RMS_EPS = 1e-6
L2_EPS = 1e-6

SB_W = SB_HEADS * SB_HEAD_DIM
GDN_QK_W = GDN_HEADS * GDN_K_DIM
GDN_V_W = GDN_HEADS * GDN_V_DIM
GDN_CONV_W = 2 * GDN_QK_W + GDN_V_W
MEM_W = MEM_HEADS * MEM_HEAD_DIM
N_BRANCH = 3
IN_SPLITS = (SB_W, SB_W, SB_W, GDN_CONV_W, GDN_V_W, GDN_HEADS, GDN_HEADS, MEM_W, N_BRANCH * D_MODEL)
D_IN = sum(IN_SPLITS)

kernel_name = 'hybrid_sb_gdn_mem_moe_decode_step'


def _split(x, sizes):
    offs = np.cumsum(np.array(sizes))[:-1].tolist()
    return jnp.split(x, offs, axis=-1)


def _heads(t, n):
    return t.reshape(t.shape[0], t.shape[1], n, -1)


def layer_norm(x, g, b):
    xf = x.astype(jnp.float32)
    mu = jnp.mean(xf, axis=-1, keepdims=True)
    var = jnp.mean(jnp.square(xf - mu), axis=-1, keepdims=True)
    return ((xf - mu) * lax.rsqrt(var + LN_EPS) * g + b).astype(x.dtype)


def l2norm(x):
    xf = x.astype(jnp.float32)
    return xf * lax.rsqrt(jnp.sum(xf * xf, axis=-1, keepdims=True) + L2_EPS)


def _sb_block(q, q_pos, k, v, k_pos, bias):
    z = jnp.einsum('bqhd,bkhd->bhqk', q, k, preferred_element_type=jnp.float32) * (SB_HEAD_DIM ** -0.5)
    z = z + bias.astype(jnp.float32)[None, :, None, None]
    causal = k_pos[None, :] < q_pos[:, None]
    log_keep = jnp.where(causal, jax.nn.log_sigmoid(-z), 0.0)
    incl = lax.cumsum(log_keep, axis=3, reverse=True)
    excl = jnp.concatenate([incl[..., 1:], jnp.zeros_like(incl[..., :1])], axis=-1)
    a = jnp.where(causal, jnp.exp(jax.nn.log_sigmoid(z) + excl), 0.0)
    return jnp.einsum('bhqk,bkhd->bqhd', a, v.astype(jnp.float32))


def stick_breaking(q, k, v, start, bias):
    B, L, H, Dh = q.shape
    qb = math.gcd(L, SB_BLOCK)
    nb = L // qb
    k_pos = jnp.arange(k.shape[1], dtype=jnp.int32)
    q_pos = (start + jnp.arange(L, dtype=jnp.int32)).reshape(nb, qb)
    q_blocks = q.reshape(B, nb, qb, H, Dh).transpose(1, 0, 2, 3, 4)
    out = lax.map(lambda a: _sb_block(a[0], a[1], k, v, k_pos, bias), (q_blocks, q_pos))
    return out.transpose(1, 0, 2, 3, 4).reshape(B, L, H, Dh).astype(q.dtype)


def causal_conv(x, buf, w):
    L = x.shape[1]
    xp = jnp.concatenate([buf.astype(x.dtype), x], axis=1)
    y = sum(xp[:, i:i + L] * w[i] for i in range(GDN_CONV))
    return jax.nn.silu(y), xp[:, -(GDN_CONV - 1):]


def gated_delta_rule(q, k, v, g, beta, s0):
    B, L, H, Dk = q.shape
    Dv = v.shape[-1]
    C = math.gcd(L, GDN_CHUNK)
    N = L // C

    def chunks(x):
        return x.reshape(B, N, C, H, -1).transpose(1, 0, 3, 2, 4)

    qc, kc, vc = chunks(q * (Dk ** -0.5)), chunks(k), chunks(v)
    gc = jnp.cumsum(g.reshape(B, N, C, H).transpose(1, 0, 3, 2), axis=-1)
    bc = beta.reshape(B, N, C, H).transpose(1, 0, 3, 2)[..., None]
    tril = jnp.tril(jnp.ones((C, C), bool))
    strict = jnp.tril(jnp.ones((C, C), bool), -1)
    decay = jnp.exp(jnp.where(tril, gc[..., :, None] - gc[..., None, :], -jnp.inf))
    kk = jnp.einsum('nbhid,nbhjd->nbhij', kc * bc, kc) * decay
    a_mat = jnp.eye(C, dtype=jnp.float32) + jnp.where(strict, kk, 0.0)
    rhs = jnp.concatenate([vc * bc, kc * bc * jnp.exp(gc)[..., None]], axis=-1)
    sol = lax.linalg.triangular_solve(a_mat, rhs, left_side=True, lower=True, unit_diagonal=True)
    u, w = sol[..., :Dv], sol[..., Dv:]
    qk = jnp.where(tril, jnp.einsum('nbhid,nbhjd->nbhij', qc, kc) * decay, 0.0)
    q_dec = qc * jnp.exp(gc)[..., None]
    k_tail = kc * jnp.exp(gc[..., -1:] - gc)[..., None]
    g_last = jnp.exp(gc[..., -1])

    def step(S, inp):
        u_i, w_i, qk_i, qd_i, kt_i, gl_i = inp
        v_new = u_i - jnp.einsum('bhcd,bhde->bhce', w_i, S)
        o = jnp.einsum('bhcd,bhde->bhce', qd_i, S) + jnp.einsum('bhij,bhje->bhie', qk_i, v_new)
        S = S * gl_i[..., None, None] + jnp.einsum('bhcd,bhce->bhde', kt_i, v_new)
        return S, o

    S, o = lax.scan(step, s0, (u, w, qk, q_dec, k_tail, g_last))
    return o.transpose(1, 0, 3, 2, 4).reshape(B, L, H, Dv), S


def memory_attention(q, mem_k, mem_v):
    s = jnp.einsum('blhd,bmhd->bhlm', q, mem_k, preferred_element_type=jnp.float32) * (MEM_HEAD_DIM ** -0.5)
    p = jax.nn.softmax(s, axis=-1)
    return jnp.einsum('bhlm,bmhd->blhd', p, mem_v.astype(jnp.float32)).astype(q.dtype)


def moe(x2d, w_router, router_bias, w_exp_gate, w_exp_up, w_exp_down, w_sh_gate, w_sh_up, w_sh_down):
    T, D = x2d.shape
    scores = jax.nn.sigmoid(jnp.dot(x2d, w_router, preferred_element_type=jnp.float32))
    biased = scores + router_bias.astype(jnp.float32)
    grp_score = lax.top_k(biased.reshape(T, N_GROUPS, N_EXPERTS // N_GROUPS), 2)[0].sum(-1)
    _, grp_idx = lax.top_k(grp_score, TOPK_GROUPS)
    grp_mask = jnp.any(grp_idx[..., None] == jnp.arange(N_GROUPS), axis=1)
    choice = jnp.where(jnp.repeat(grp_mask, N_EXPERTS // N_GROUPS, axis=1), biased, -jnp.inf)
    _, top_idx = lax.top_k(choice, TOP_K)
    top_w = jnp.take_along_axis(scores, top_idx, axis=1)
    top_w = top_w / jnp.sum(top_w, axis=-1, keepdims=True) * ROUTE_SCALE

    TK = T * TOP_K
    flat_e = top_idx.reshape(TK)
    flat_t = jnp.repeat(jnp.arange(T, dtype=jnp.int32), TOP_K)
    flat_w = top_w.reshape(TK)
    order = jnp.argsort(flat_e)
    sorted_e = flat_e[order]
    counts = jnp.bincount(flat_e, length=N_EXPERTS)
    padded = (counts + MOE_BLOCK - 1) // MOE_BLOCK * MOE_BLOCK
    start = jnp.cumsum(counts) - counts
    pend = jnp.cumsum(padded)
    pstart = pend - padded
    dest = pstart[sorted_e] + jnp.arange(TK) - start[sorted_e]
    n_blocks = -(-(TK + N_EXPERTS * (MOE_BLOCK - 1)) // MOE_BLOCK)
    n_rows = n_blocks * MOE_BLOCK
    row_tok = jnp.full((n_rows,), T, jnp.int32).at[dest].set(flat_t[order])
    row_w = jnp.zeros((n_rows,), jnp.float32).at[dest].set(flat_w[order])
    block_e = jnp.minimum(jnp.searchsorted(pend, jnp.arange(n_blocks) * MOE_BLOCK, side='right'), N_EXPERTS - 1)
    x_pad = jnp.concatenate([x2d, jnp.zeros((1, D), x2d.dtype)], axis=0)

    def expert_block(args):
        tok, e = args
        xb = x_pad[tok]
        h = jax.nn.silu(xb @ w_exp_gate[e]) * (xb @ w_exp_up[e])
        return h @ w_exp_down[e]

    yb = lax.map(expert_block, (row_tok.reshape(n_blocks, MOE_BLOCK), block_e))
    routed = jnp.zeros((T + 1, D), jnp.float32).at[row_tok].add(
        yb.reshape(n_rows, D).astype(jnp.float32) * row_w[:, None])[:T]
    shared = (jax.nn.silu(x2d @ w_sh_gate) * (x2d @ w_sh_up)) @ w_sh_down
    return (routed + shared.astype(jnp.float32)).astype(x2d.dtype)


def decoder_layer(x, past_k, past_v, s0, conv0, mem_k, mem_v, w_in, sb_logit_bias, conv_w, gdn_A_log, gdn_dt_bias,
                  gdn_norm_w, w_br_sb, w_br_gdn, w_br_mem, w_o, ln1_g, ln1_b, w_router, router_bias, w_exp_gate,
                  w_exp_up, w_exp_down, w_sh_gate, w_sh_up, w_sh_down, ln2_g, ln2_b):
    B, L, D = x.shape
    start = past_k.shape[1]
    sb_q, sb_k, sb_v, gdn_qkv, gdn_z, gdn_b, gdn_a, mem_q, gates = _split(x @ w_in, IN_SPLITS)

    sb_k = _heads(sb_k, SB_HEADS)
    sb_v = _heads(sb_v, SB_HEADS)
    k_all = jnp.concatenate([past_k.astype(x.dtype), sb_k], axis=1)
    v_all = jnp.concatenate([past_v.astype(x.dtype), sb_v], axis=1)
    o_sb = stick_breaking(_heads(sb_q, SB_HEADS), k_all, v_all, start, sb_logit_bias).reshape(B, L, SB_W)

    conv_out, conv_new = causal_conv(gdn_qkv, conv0, conv_w)
    g_q, g_k, g_v = _split(conv_out, (GDN_QK_W, GDN_QK_W, GDN_V_W))
    beta = jax.nn.sigmoid(gdn_b.astype(jnp.float32))
    g = -jnp.exp(gdn_A_log.astype(jnp.float32)) * jax.nn.softplus(gdn_a.astype(jnp.float32) + gdn_dt_bias)
    o_gdn, s_new = gated_delta_rule(l2norm(_heads(g_q, GDN_HEADS)), l2norm(_heads(g_k, GDN_HEADS)),
                                    _heads(g_v, GDN_HEADS).astype(jnp.float32), g, beta, s0.astype(jnp.float32))
    o_gdn = o_gdn * lax.rsqrt(jnp.mean(o_gdn * o_gdn, axis=-1, keepdims=True) + RMS_EPS) * gdn_norm_w
    o_gdn = (o_gdn * jax.nn.silu(_heads(gdn_z, GDN_HEADS).astype(jnp.float32))).astype(x.dtype).reshape(B, L, GDN_V_W)

    o_mem = memory_attention(_heads(mem_q, MEM_HEADS), mem_k, mem_v).reshape(B, L, MEM_W)

    g_sb, g_gdn, g_mem = _split(jax.nn.sigmoid(gates), (D_MODEL, D_MODEL, D_MODEL))
    merged = g_sb * (o_sb @ w_br_sb) + g_gdn * (o_gdn @ w_br_gdn) + g_mem * (o_mem @ w_br_mem)
    h = layer_norm(DN_ALPHA * x + merged @ w_o, ln1_g, ln1_b)

    f = moe(h.reshape(B * L, D), w_router, router_bias, w_exp_gate, w_exp_up, w_exp_down,
            w_sh_gate, w_sh_up, w_sh_down).reshape(B, L, D)
    y = layer_norm(DN_ALPHA * h + f, ln2_g, ln2_b)
    return y, sb_k, sb_v, s_new.astype(s0.dtype), conv_new.astype(conv0.dtype)


def setup_inputs(seed: int = 0) -> dict:
    key = jax.random.key(seed)
    k = jax.random.split(key, 40)
    f32 = jnp.float32

    def nrm(i, shape, scale=1.0):
        return jax.random.normal(k[i], shape, f32) * scale

    n_pages = PAST_LEN // PAGE_SIZE
    n_used = DEC_BATCH * n_pages
    n_pool = n_used + max(1, n_used // 4)
    page_table = jax.random.permutation(k[0], n_pool)[:n_used].reshape(DEC_BATCH, n_pages).astype(jnp.int32)
    dt = jnp.exp(jax.random.uniform(k[1], (GDN_HEADS,), f32, math.log(1e-3), math.log(1e-1)))
    gdn_dt_bias = dt + jnp.log(-jnp.expm1(-dt))
    gdn_A_log = jnp.log(jax.random.uniform(k[2], (GDN_HEADS,), f32, 1.0, 16.0))
    return {
        'x_prompt': nrm(3, (BATCH, SEQ, D_MODEL)),
        'x_sample': nrm(4, (DEC_BATCH, DEC_SEQ, D_MODEL)),
        'cache_sb_k': nrm(5, (n_pool, PAGE_SIZE, SB_HEADS, SB_HEAD_DIM)),
        'cache_sb_v': nrm(6, (n_pool, PAGE_SIZE, SB_HEADS, SB_HEAD_DIM)),
        'page_table': page_table,
        'state_gdn': nrm(7, (DEC_BATCH, GDN_HEADS, GDN_K_DIM, GDN_V_DIM), GDN_K_DIM ** -0.5),
        'state_gdn_conv': nrm(8, (DEC_BATCH, GDN_CONV - 1, GDN_CONV_W)),
        'cache_mem_k': nrm(9, (DEC_BATCH, N_MEM, MEM_HEADS, MEM_HEAD_DIM)),
        'cache_mem_v': nrm(10, (DEC_BATCH, N_MEM, MEM_HEADS, MEM_HEAD_DIM)),
        'mem_prompt': nrm(11, (BATCH, N_MEM, D_MODEL)),
        'w_in': nrm(12, (D_MODEL, D_IN), D_MODEL ** -0.5),
        'sb_logit_bias': SB_BIAS_INIT + nrm(33, (SB_HEADS,), 0.1),
        'conv_w': nrm(13, (GDN_CONV, GDN_CONV_W), GDN_CONV ** -0.5),
        'gdn_A_log': gdn_A_log,
        'gdn_dt_bias': gdn_dt_bias,
        'gdn_norm_w': 1.0 + nrm(14, (GDN_V_DIM,), 0.02),
        'w_mem_k': nrm(15, (D_MODEL, MEM_W), D_MODEL ** -0.5),
        'w_mem_v': nrm(16, (D_MODEL, MEM_W), D_MODEL ** -0.5),
        'w_br_sb': nrm(17, (SB_W, D_MODEL), SB_W ** -0.5 * DN_BETA),
        'w_br_gdn': nrm(18, (GDN_V_W, D_MODEL), GDN_V_W ** -0.5 * DN_BETA),
        'w_br_mem': nrm(19, (MEM_W, D_MODEL), MEM_W ** -0.5 * DN_BETA),
        'w_o': nrm(20, (D_MODEL, D_MODEL), D_MODEL ** -0.5 * DN_BETA),
        'ln1_g': 1.0 + nrm(21, (D_MODEL,), 0.02),
        'ln1_b': nrm(22, (D_MODEL,), 0.02),
        'w_router': nrm(23, (D_MODEL, N_EXPERTS), D_MODEL ** -0.5),
        'router_bias': nrm(24, (N_EXPERTS,), 0.01),
        'w_exp_gate': nrm(25, (N_EXPERTS, D_MODEL, D_EXPERT), D_MODEL ** -0.5),
        'w_exp_up': nrm(26, (N_EXPERTS, D_MODEL, D_EXPERT), D_MODEL ** -0.5),
        'w_exp_down': nrm(27, (N_EXPERTS, D_EXPERT, D_MODEL), D_EXPERT ** -0.5 * DN_BETA),
        'w_sh_gate': nrm(28, (D_MODEL, D_SHARED), D_MODEL ** -0.5),
        'w_sh_up': nrm(29, (D_MODEL, D_SHARED), D_MODEL ** -0.5),
        'w_sh_down': nrm(30, (D_SHARED, D_MODEL), D_SHARED ** -0.5 * DN_BETA),
        'ln2_g': 1.0 + nrm(31, (D_MODEL,), 0.02),
        'ln2_b': nrm(32, (D_MODEL,), 0.02),
    }


def reference(x_prompt, x_sample, cache_sb_k, cache_sb_v, page_table, state_gdn, state_gdn_conv, cache_mem_k,
              cache_mem_v, mem_prompt, w_in, sb_logit_bias, conv_w, gdn_A_log, gdn_dt_bias, gdn_norm_w, w_mem_k,
              w_mem_v, w_br_sb, w_br_gdn, w_br_mem, w_o, ln1_g, ln1_b, w_router, router_bias, w_exp_gate, w_exp_up,
              w_exp_down, w_sh_gate, w_sh_up, w_sh_down, ln2_g, ln2_b):
    weights = (w_in, sb_logit_bias, conv_w, gdn_A_log, gdn_dt_bias, gdn_norm_w, w_br_sb, w_br_gdn, w_br_mem, w_o,
               ln1_g, ln1_b, w_router, router_bias, w_exp_gate, w_exp_up, w_exp_down, w_sh_gate, w_sh_up, w_sh_down,
               ln2_g, ln2_b)

    B = x_prompt.shape[0]
    dt = x_prompt.dtype
    mem_k_p = (mem_prompt @ w_mem_k).reshape(B, N_MEM, MEM_HEADS, MEM_HEAD_DIM)
    mem_v_p = (mem_prompt @ w_mem_v).reshape(B, N_MEM, MEM_HEADS, MEM_HEAD_DIM)
    empty_kv = jnp.zeros((B, 0, SB_HEADS, SB_HEAD_DIM), dt)
    s0 = jnp.zeros((B, GDN_HEADS, GDN_K_DIM, GDN_V_DIM), dt)
    c0 = jnp.zeros((B, GDN_CONV - 1, GDN_CONV_W), dt)
    y_prompt, sb_k_p, sb_v_p, s_p, c_p = decoder_layer(x_prompt, empty_kv, empty_kv, s0, c0, mem_k_p, mem_v_p, *weights)

    DB, n_pages = page_table.shape
    past_k = cache_sb_k[page_table].reshape(DB, n_pages * PAGE_SIZE, SB_HEADS, SB_HEAD_DIM)
    past_v = cache_sb_v[page_table].reshape(DB, n_pages * PAGE_SIZE, SB_HEADS, SB_HEAD_DIM)
    y_sample, sb_k_s, sb_v_s, s_s, c_s = decoder_layer(x_sample, past_k, past_v, state_gdn, state_gdn_conv,
                                                       cache_mem_k, cache_mem_v, *weights)
    return (y_prompt, y_sample, sb_k_p, sb_v_p, s_p, c_p, mem_k_p, mem_v_p, sb_k_s, sb_v_s, s_s, c_s)
```

```python
import functools

import jax
import jax.numpy as jnp
from jax import lax
from jax.experimental import pallas as pl
from jax.experimental.pallas import tpu as pltpu

F32 = jnp.float32
BF16 = jnp.bfloat16
I32 = jnp.int32

D_MODEL = 1024
PAGE_SIZE = 128
SB_HEADS = 8
SB_HEAD_DIM = 64
SB_W = SB_HEADS * SB_HEAD_DIM
GDN_HEADS = 8
GDN_K_DIM = 64
GDN_V_DIM = 64
GDN_CONV = 4
GDN_QK_W = GDN_HEADS * GDN_K_DIM
GDN_V_W = GDN_HEADS * GDN_V_DIM
GDN_CONV_W = 2 * GDN_QK_W + GDN_V_W
N_MEM = 256
MEM_HEADS = 4
MEM_HEAD_DIM = 128
MEM_W = MEM_HEADS * MEM_HEAD_DIM
N_EXPERTS = 64
TOP_K = 8
N_GROUPS = 8
TOPK_GROUPS = 4
D_EXPERT = 256
D_SHARED = 256
ROUTE_SCALE = 2.5
DEPTH = 1
DN_ALPHA = (2 * DEPTH) ** 0.25
LN_EPS = 1e-5
RMS_EPS = 1e-6
L2_EPS = 1e-6

LANES = 128
CHUNK = 128
ROW_BLOCK = 256
VMEM_LIMIT = 56 * 1024 * 1024


def _cparams(*sem):
    return pltpu.CompilerParams(dimension_semantics=sem, vmem_limit_bytes=VMEM_LIMIT)


def _dot(a, b):
    return jnp.dot(a, b, preferred_element_type=F32)


def _dot_nt(a, b):
    return lax.dot_general(a, b, (((1,), (1,)), ((), ())), preferred_element_type=F32)


def _split2(x):
    hi = x.astype(BF16)
    lo = (x - hi.astype(F32)).astype(BF16)
    return hi, lo


def _split3(x):
    hi = x.astype(BF16)
    r = x - hi.astype(F32)
    mid = r.astype(BF16)
    lo = (r - mid.astype(F32)).astype(BF16)
    return hi, mid, lo


def _dot_xc(x, m):
    hi, mid, lo = _split3(x)
    return _dot(hi, m) + _dot(mid, m) + _dot(lo, m)


def _dot_cx(m, x):
    hi, mid, lo = _split3(x)
    return _dot(m, hi) + _dot(m, mid) + _dot(m, lo)


def _dot_ff(a, b):
    ah, al = _split2(a)
    bh, bl = _split2(b)
    return _dot(ah, bh) + _dot(ah, bl) + _dot(al, bh)


def _sigmoid(x):
    return 1.0 / (1.0 + jnp.exp(-x))


def _softplus(x):
    return jnp.maximum(x, 0.0) + jnp.log(1.0 + jnp.exp(-jnp.abs(x)))


def _iota(shape, dim):
    return lax.broadcasted_iota(I32, shape, dim)


def _const_spec(shape):
    nd = len(shape)
    return pl.BlockSpec(shape, lambda *_: (0,) * nd, pipeline_mode=pl.Buffered(1))


def _layer_norm(x, g, b):
    mu = jnp.mean(x, axis=-1, keepdims=True)
    xc = x - mu
    var = jnp.mean(xc * xc, axis=-1, keepdims=True)
    return xc * lax.rsqrt(var + LN_EPS) * g + b


def _proj_kernel(x_ref, w_ref, wba_ref, q_ref, k_ref, v_ref, g_ref, z_ref, mq_ref, ba_ref):
    xb = x_ref[...].astype(BF16)
    q_ref[...] = (_dot(xb, w_ref[:, 0:512]) * 0.125).astype(BF16)
    k_ref[...] = _dot(xb, w_ref[:, 512:1024])
    v_ref[...] = _dot(xb, w_ref[:, 1024:1536])
    for c in range(3):
        g_ref[:, c * 512:(c + 1) * 512] = _dot(xb, w_ref[:, 1536 + c * 512:2048 + c * 512])
    z_ref[...] = _dot(xb, w_ref[:, 3072:3584]).astype(BF16)
    mq_ref[...] = _dot(xb, w_ref[:, 3584:4096]).astype(BF16)
    ba_ref[...] = _dot(xb, wba_ref[...])


def _proj_in(x2d, w_main, w_ba, tm):
    t = x2d.shape[0]
    row = lambda w: pl.BlockSpec((tm, w), lambda i: (i, 0))
    return pl.pallas_call(
        _proj_kernel,
        grid=(t // tm,),
        in_specs=[row(D_MODEL), _const_spec((D_MODEL, 4096)), _const_spec((D_MODEL, LANES))],
        out_specs=[row(512), row(512), row(512), row(1536), row(512), row(512), row(LANES)],
        out_shape=[jax.ShapeDtypeStruct((t, 512), BF16), jax.ShapeDtypeStruct((t, 512), F32),
                   jax.ShapeDtypeStruct((t, 512), F32), jax.ShapeDtypeStruct((t, 1536), F32),
                   jax.ShapeDtypeStruct((t, 512), BF16), jax.ShapeDtypeStruct((t, 512), BF16),
                   jax.ShapeDtypeStruct((t, LANES), F32)],
        compiler_params=_cparams("parallel"),
        name="proj_in",
    )(x2d, w_main, w_ba)


def _memkv_kernel(x_ref, w_ref, k_ref, v_ref):
    xb = x_ref[...].astype(BF16)
    k_ref[...] = _dot(xb, w_ref[:, 0:512])
    v_ref[...] = _dot(xb, w_ref[:, 512:1024])


def _mem_kv(x2d, w_kv, tm):
    t = x2d.shape[0]
    row = lambda w: pl.BlockSpec((tm, w), lambda i: (i, 0))
    return pl.pallas_call(
        _memkv_kernel,
        grid=(t // tm,),
        in_specs=[row(D_MODEL), _const_spec((D_MODEL, 1024))],
        out_specs=[row(512), row(512)],
        out_shape=[jax.ShapeDtypeStruct((t, 512), F32)] * 2,
        compiler_params=_cparams("parallel"),
        name="mem_kv",
    )(x2d, w_kv)


def _cumsum_matrix():
    r = _iota((CHUNK, 2 * CHUNK), 0)
    c = _iota((CHUNK, 2 * CHUNK), 1)
    return jnp.where((c >= CHUNK) | (r > c), 1.0, 0.0).astype(BF16)


def _sb_block(qh, kj, vj, bias, to, carry, causal):
    z = _dot_nt(qh, kj) + bias
    sp = _softplus(z)
    lk = -sp
    if causal is not None:
        lk = jnp.where(causal, lk, 0.0)
    hi, lo = _split2(lk)
    r = _dot(hi, to) + _dot(lo, to)
    excl = r[:, :CHUNK]
    tot = r[:, CHUNK:]
    a = jnp.exp(z - sp + excl + carry)
    if causal is not None:
        a = jnp.where(causal, a, 0.0)
    return _dot(a.astype(BF16), vj), tot


def _sb_prompt_kernel(bias_ref, q_ref, k_ref, v_ref, o_ref, q0s, q1s, kbs, v0s, v1s):
    hp = pl.program_id(1)
    seq = q_ref.shape[1]
    nblk = seq // CHUNK
    lane = _iota((CHUNK, LANES), 1)
    first = lane < SB_HEAD_DIM
    causal = lane < _iota((CHUNK, LANES), 0)
    to = _cumsum_matrix()
    b0 = bias_ref[2 * hp]
    b1 = bias_ref[2 * hp + 1]

    def stage(i, _):
        r0 = pl.multiple_of(i * CHUNK, CHUNK)
        rows = pl.ds(r0, CHUNK)
        qf = q_ref[0, rows, :].astype(F32)
        q0s[rows, :] = jnp.where(first, qf, 0.0).astype(BF16)
        q1s[rows, :] = jnp.where(first, 0.0, qf).astype(BF16)
        kbs[rows, :] = k_ref[0, rows, :].astype(BF16)
        vf = v_ref[0, rows, :]
        v0s[rows, :] = jnp.where(first, vf, 0.0).astype(BF16)
        v1s[rows, :] = jnp.where(first, 0.0, vf).astype(BF16)
        return 0

    lax.fori_loop(0, nblk, stage, 0)

    def qblock(i, _):
        rows = pl.ds(pl.multiple_of(i * CHUNK, CHUNK), CHUNK)
        q0 = q0s[rows, :]
        q1 = q1s[rows, :]
        kd = kbs[rows, :]
        zero = jnp.zeros((CHUNK, LANES), F32)
        o0, c0 = _sb_block(q0, kd, v0s[rows, :], b0, to, zero, causal)
        o1, c1 = _sb_block(q1, kd, v1s[rows, :], b1, to, zero, causal)

        def kblock(t, carry):
            acc, c0, c1 = carry
            cols = pl.ds(pl.multiple_of((i - 1 - t) * CHUNK, CHUNK), CHUNK)
            kj = kbs[cols, :]
            o0, t0 = _sb_block(q0, kj, v0s[cols, :], b0, to, c0, None)
            o1, t1 = _sb_block(q1, kj, v1s[cols, :], b1, to, c1, None)
            return acc + o0 + o1, c0 + t0, c1 + t1

        acc, _, _ = lax.fori_loop(0, i, kblock, (o0 + o1, c0, c1))
        o_ref[0, rows, :] = acc.astype(BF16)
        return 0

    lax.fori_loop(0, nblk, qblock, 0)


def _sb_prompt(q, k, v, bias):
    b, seq, _ = q.shape
    blk = lambda: pl.BlockSpec((1, seq, LANES), lambda i, j: (i, 0, j))
    return pl.pallas_call(
        _sb_prompt_kernel,
        grid=(b, SB_W // LANES),
        in_specs=[pl.BlockSpec(memory_space=pltpu.SMEM), blk(), blk(), blk()],
        out_specs=blk(),
        out_shape=jax.ShapeDtypeStruct((b, seq, SB_W), BF16),
        scratch_shapes=[pltpu.VMEM((seq, LANES), BF16)] * 5,
        compiler_params=_cparams("parallel", "parallel"),
        name="sb_prompt",
    )(bias, q, k, v)


def _sb_sample_kernel(n_pages, pt_ref, q_ref, bias_ref, *refs):
    del pt_ref
    k_refs = refs[:n_pages]
    v_refs = refs[n_pages:2 * n_pages]
    o_ref = refs[2 * n_pages]
    head_of_lane = _iota((SB_HEADS, SB_W), 1) // SB_HEAD_DIM
    own = head_of_lane == _iota((SB_HEADS, SB_W), 0)
    qm = jnp.where(own, q_ref[0].astype(F32), 0.0).astype(BF16)
    bias = bias_ref[...]
    to = _cumsum_matrix()
    carry = jnp.zeros((SB_HEADS, LANES), F32)
    acc = jnp.zeros((SB_HEADS, SB_W), F32)
    for p in reversed(range(n_pages)):
        kp = k_refs[p][0].astype(BF16)
        vp = v_refs[p][0].astype(BF16)
        o, tot = _sb_block(qm, kp, vp, bias, to, carry, None)
        acc = acc + o
        carry = carry + tot
    o_ref[0] = jnp.sum(jnp.where(own, acc, 0.0), axis=0, keepdims=True).astype(BF16)


def _sb_sample(q, cache_k, cache_v, page_table, bias):
    db, n_pages = page_table.shape
    n_pool = cache_k.shape[0]
    ck = cache_k.reshape(n_pool, PAGE_SIZE, SB_W)
    cv = cache_v.reshape(n_pool, PAGE_SIZE, SB_W)
    bias2d = jnp.broadcast_to(bias.astype(F32)[:, None], (SB_HEADS, LANES))

    def page_spec(p):
        return pl.BlockSpec((1, PAGE_SIZE, SB_W), lambda i, pt: (pt[i, p], 0, 0))

    grid_spec = pltpu.PrefetchScalarGridSpec(
        num_scalar_prefetch=1,
        grid=(db,),
        in_specs=[pl.BlockSpec((1, 1, SB_W), lambda i, pt: (i, 0, 0)),
                  pl.BlockSpec((SB_HEADS, LANES), lambda i, pt: (0, 0))]
        + [page_spec(p) for p in range(n_pages)] * 2,
        out_specs=pl.BlockSpec((1, 1, SB_W), lambda i, pt: (i, 0, 0)),
    )
    out = pl.pallas_call(
        functools.partial(_sb_sample_kernel, n_pages),
        grid_spec=grid_spec,
        out_shape=jax.ShapeDtypeStruct((db, 1, SB_W), BF16),
        compiler_params=_cparams("parallel"),
        name="sb_sample",
    )(page_table, q.reshape(db, 1, SB_W), bias2d, *([ck] * n_pages), *([cv] * n_pages))
    return out.reshape(db, SB_W)


def _head_ones():
    r = _iota((LANES, LANES), 0) // GDN_K_DIM
    c = _iota((LANES, LANES), 1) // GDN_K_DIM
    return jnp.where(r == c, 1.0, 0.0).astype(BF16)


def _unit_lower_inverse(a):
    eye = jnp.where(_iota((CHUNK, CHUNK), 0) == _iota((CHUNK, CHUNK), 1), 1.0, 0.0)
    p = -a
    x = eye + p
    for _ in range(6):
        p = _dot_ff(p, p)
        x = x + _dot_ff(x, p)
    return x


def _gdn_prompt_kernel(qp_ref, kp_ref, vp_ref, z_ref, ba_ref, cwq_ref, cwk_ref, cwv_ref, alog_ref, dtb_ref,
                       nw_ref, o_ref, s_ref, qs, ks, vs):
    hp = pl.program_id(1)
    seq = qp_ref.shape[1]
    pad = 8
    for scr, src in ((qs, qp_ref), (ks, kp_ref), (vs, vp_ref)):
        scr[0:pad, :] = jnp.zeros((pad, LANES), F32)
        scr[pad:seq + pad, :] = src[0]

    row = _iota((CHUNK, CHUNK), 0)
    col = _iota((CHUNK, CHUNK), 1)
    lower = row >= col
    strict = row > col
    first = col < GDN_K_DIM
    same_head = (row // GDN_K_DIM) == (col // GDN_K_DIM)
    head_ones = _head_ones()
    lower_ones = jnp.where(lower, 1.0, 0.0).astype(BF16)
    er = _iota((LANES, 4 * LANES), 0)
    ec = _iota((LANES, 4 * LANES), 1)
    sec = ec // LANES
    pair_head = 2 * hp + (ec % LANES) // GDN_K_DIM
    want = jnp.where(sec == 0, pair_head,
                     jnp.where(sec == 1, GDN_HEADS + pair_head,
                               jnp.where(sec == 2, GDN_HEADS + 2 * hp, GDN_HEADS + 2 * hp + 1)))
    expand = jnp.where(er == want, 1.0, 0.0).astype(BF16)
    lane_row = _iota((1, LANES), 1)
    neg_a = -jnp.exp(alog_ref[...])
    scale = GDN_K_DIM ** -0.5

    def conv(scr, cw_ref, r0):
        w = scr[pl.ds(r0, CHUNK + pad), :]
        y = (w[5:5 + CHUNK] * cw_ref[0:1, :] + w[6:6 + CHUNK] * cw_ref[1:2, :]
             + w[7:7 + CHUNK] * cw_ref[2:3, :] + w[8:8 + CHUNK] * cw_ref[3:4, :])
        return y * _sigmoid(y)

    def l2n(y):
        ss = _dot_xc(y * y, head_ones)
        return y * lax.rsqrt(ss + L2_EPS)

    def chunk(c, s):
        r0 = pl.multiple_of(c * CHUNK, CHUNK)
        rows = pl.ds(r0, CHUNK)
        qn = l2n(conv(qs, cwq_ref, r0)) * scale
        kn = l2n(conv(ks, cwk_ref, r0))
        v = conv(vs, cwv_ref, r0)
        raw = ba_ref[0, rows, :]
        beta_raw = _sigmoid(raw)
        g_raw = neg_a * _softplus(raw + dtb_ref[...])
        gcum = _dot_cx(lower_ones, g_raw)
        comb = jnp.where(lane_row < GDN_HEADS, beta_raw, gcum)
        ex = _dot_xc(comb, expand)
        beta_p = ex[:, 0:LANES]
        gc_p = ex[:, LANES:2 * LANES]
        eg_p = jnp.exp(gc_p)
        g_last = gc_p[CHUNK - 1:CHUNK, :]
        kb = kn * beta_p
        kn_b = kn.astype(BF16)
        rhs = jnp.concatenate([v * beta_p, kb * eg_p], axis=1)
        sols = []
        qks = []
        for h in range(2):
            mine = first if h == 0 else jnp.logical_not(first)
            g1 = ex[:, (2 + h) * LANES:(3 + h) * LANES]
            diff = g1 - g1.T
            dec = jnp.where(lower, jnp.exp(jnp.where(lower, diff, 0.0)), 0.0)
            kk = _dot_nt(jnp.where(mine, kb, 0.0).astype(BF16), kn_b)
            t_inv = _unit_lower_inverse(jnp.where(strict, kk * dec, 0.0))
            sols.append(_dot_ff(t_inv, rhs))
            qks.append(_dot_nt(jnp.where(mine, qn, 0.0).astype(BF16), kn_b) * dec)
        first2 = jnp.concatenate([first, first], axis=1)
        sol = jnp.where(first2, sols[0], sols[1])
        u = sol[:, 0:LANES]
        w = sol[:, LANES:2 * LANES]
        s_b = s.astype(BF16)
        v_new = u - _dot(w.astype(BF16), s_b)
        o = _dot((qn * eg_p).astype(BF16), s_b)
        o = o + _dot(qks[0].astype(BF16), jnp.where(first, v_new, 0.0).astype(BF16))
        o = o + _dot(qks[1].astype(BF16), jnp.where(first, 0.0, v_new).astype(BF16))
        k_tail = kn * jnp.exp(g_last - gc_p)
        upd = _dot(k_tail.T.astype(BF16), v_new.astype(BF16))
        s = s * jnp.exp(g_last) + jnp.where(same_head, upd, 0.0)
        ms = _dot_xc(o * o, head_ones) * (1.0 / GDN_V_DIM)
        zf = z_ref[0, rows, :].astype(F32)
        o = o * lax.rsqrt(ms + RMS_EPS) * nw_ref[...] * (zf * _sigmoid(zf))
        o_ref[0, rows, :] = o.astype(BF16)
        return s

    s = lax.fori_loop(0, seq // CHUNK, chunk, jnp.zeros((LANES, LANES), F32))
    s_ref[0, 0] = s[0:GDN_K_DIM, 0:GDN_V_DIM]
    s_ref[0, 1] = s[GDN_K_DIM:, GDN_V_DIM:]


def _gdn_prompt(gqkv, z, ba, conv_w, alog_lane, dtb_lane, nw_lane):
    b, seq, _ = gqkv.shape
    n_pair = GDN_HEADS // 2
    blk = lambda off: pl.BlockSpec((1, seq, LANES), lambda i, j: (i, 0, off + j))
    cw = lambda off: pl.BlockSpec((GDN_CONV, LANES), lambda i, j: (0, off + j))
    lane_vec = pl.BlockSpec((1, LANES), lambda i, j: (0, 0))
    return pl.pallas_call(
        _gdn_prompt_kernel,
        grid=(b, n_pair),
        in_specs=[blk(0), blk(n_pair), blk(2 * n_pair), blk(0),
                  pl.BlockSpec((1, seq, LANES), lambda i, j: (i, 0, 0)),
                  cw(0), cw(n_pair), cw(2 * n_pair), lane_vec, lane_vec,
                  pl.BlockSpec((1, LANES), lambda i, j: (0, 0))],
        out_specs=[blk(0), pl.BlockSpec((1, 2, GDN_K_DIM, GDN_V_DIM), lambda i, j: (i, j, 0, 0))],
        out_shape=[jax.ShapeDtypeStruct((b, seq, GDN_V_W), BF16),
                   jax.ShapeDtypeStruct((b, GDN_HEADS, GDN_K_DIM, GDN_V_DIM), F32)],
        scratch_shapes=[pltpu.VMEM((seq + 8, LANES), F32)] * 3,
        compiler_params=_cparams("parallel", "parallel"),
        name="gdn_prompt",
    )(gqkv, gqkv, gqkv, z, ba, conv_w, conv_w, conv_w, alog_lane, dtb_lane, nw_lane)


GDN_BT = 8


def _gdn_sample_kernel(g_ref, c0_ref, gt_ref, c0t_ref, ba_ref, bat_ref, z_ref, s_ref, cw_ref, cwt_ref,
                       alogc_ref, dtbc_ref, nw_ref, o_ref, sn_ref, o_scr):
    scale = GDN_K_DIM ** -0.5
    y = (c0_ref[:, 0, :] * cw_ref[0:1, :] + c0_ref[:, 1, :] * cw_ref[1:2, :]
         + c0_ref[:, 2, :] * cw_ref[2:3, :] + g_ref[...] * cw_ref[3:4, :])
    v_row = (y * _sigmoid(y))[:, 2 * GDN_QK_W:]
    raw = ba_ref[...]
    er = _iota((LANES, GDN_V_W), 0)
    ec = _iota((LANES, GDN_V_W), 1)
    expand = jnp.where(er == ec // GDN_V_DIM, 1.0, 0.0).astype(BF16)
    beta_row = _dot_xc(_sigmoid(raw), expand)
    u_row = v_row * beta_row
    yt = (c0t_ref[0, 0] * cwt_ref[0] + c0t_ref[0, 1] * cwt_ref[1] + c0t_ref[0, 2] * cwt_ref[2]
          + gt_ref[0] * cwt_ref[3])
    yt = yt * _sigmoid(yt)
    rawt = bat_ref[0]
    beta_c = _sigmoid(rawt[0:GDN_HEADS])
    g_c = -jnp.exp(alogc_ref[...]) * _softplus(rawt[GDN_HEADS:] + dtbc_ref[...])
    eg_c = jnp.exp(g_c)
    for h in range(GDN_HEADS):
        qh = yt[h * GDN_K_DIM:(h + 1) * GDN_K_DIM]
        kh = yt[GDN_QK_W + h * GDN_K_DIM:GDN_QK_W + (h + 1) * GDN_K_DIM]
        qn = qh * lax.rsqrt(jnp.sum(qh * qh, axis=0, keepdims=True) + L2_EPS) * scale
        kn = kh * lax.rsqrt(jnp.sum(kh * kh, axis=0, keepdims=True) + L2_EPS)
        eg = eg_c[h:h + 1, :]
        wn = kn * (beta_c[h:h + 1, :] * eg)
        qd = qn * eg
        qk = jnp.sum(qn * kn, axis=0, keepdims=True)
        for b in range(GDN_BT):
            s = s_ref[b, h]
            u = u_row[b:b + 1, h * GDN_V_DIM:(h + 1) * GDN_V_DIM]
            v_new = u - jnp.sum(wn[:, b:b + 1] * s, axis=0, keepdims=True)
            o = jnp.sum(qd[:, b:b + 1] * s, axis=0, keepdims=True) + qk[:, b:b + 1] * v_new
            sn_ref[b, h] = s * eg[:, b:b + 1] + kn[:, b:b + 1] * v_new
            o_scr[b:b + 1, h * GDN_V_DIM:(h + 1) * GDN_V_DIM] = o
    o = o_scr[...]
    r = _iota((GDN_V_W, GDN_V_W), 0) // GDN_V_DIM
    c = _iota((GDN_V_W, GDN_V_W), 1) // GDN_V_DIM
    ms = _dot_xc(o * o, jnp.where(r == c, 1.0, 0.0).astype(BF16)) * (1.0 / GDN_V_DIM)
    zf = z_ref[...].astype(F32)
    o_ref[...] = (o * lax.rsqrt(ms + RMS_EPS) * nw_ref[...] * (zf * _sigmoid(zf))).astype(BF16)


def _gdn_sample(gqkv, conv0, ba, z, state, conv_w, alog, dtb, norm_w):
    db = gqkv.shape[0]
    nt = db // GDN_BT
    qk_w = 2 * GDN_QK_W
    gt = gqkv[:, :qk_w].T.reshape(qk_w, nt, GDN_BT).transpose(1, 0, 2)
    c0t = conv0[:, :, :qk_w].transpose(1, 2, 0).reshape(3, qk_w, nt, GDN_BT).transpose(2, 0, 1, 3)
    bat = ba[:, :2 * GDN_HEADS].T.reshape(2 * GDN_HEADS, nt, GDN_BT).transpose(1, 0, 2)
    cwt = conv_w[:, :qk_w].reshape(GDN_CONV, qk_w, 1)
    alog_c = alog.astype(F32).reshape(GDN_HEADS, 1)
    dtb_c = dtb.astype(F32).reshape(GDN_HEADS, 1)
    nw = jnp.tile(norm_w.astype(F32), GDN_HEADS).reshape(1, GDN_V_W)
    full = lambda shape: pl.BlockSpec(shape, lambda i: (0,) * len(shape))
    return pl.pallas_call(
        _gdn_sample_kernel,
        grid=(nt,),
        in_specs=[pl.BlockSpec((GDN_BT, GDN_CONV_W), lambda i: (i, 0)),
                  pl.BlockSpec((GDN_BT, GDN_CONV - 1, GDN_CONV_W), lambda i: (i, 0, 0)),
                  pl.BlockSpec((1, qk_w, GDN_BT), lambda i: (i, 0, 0)),
                  pl.BlockSpec((1, GDN_CONV - 1, qk_w, GDN_BT), lambda i: (i, 0, 0, 0)),
                  pl.BlockSpec((GDN_BT, LANES), lambda i: (i, 0)),
                  pl.BlockSpec((1, 2 * GDN_HEADS, GDN_BT), lambda i: (i, 0, 0)),
                  pl.BlockSpec((GDN_BT, GDN_V_W), lambda i: (i, 0)),
                  pl.BlockSpec((GDN_BT, GDN_HEADS, GDN_K_DIM, GDN_V_DIM), lambda i: (i, 0, 0, 0)),
                  full((GDN_CONV, GDN_CONV_W)), full((GDN_CONV, qk_w, 1)),
                  full((GDN_HEADS, 1)), full((GDN_HEADS, 1)), full((1, GDN_V_W))],
        out_specs=[pl.BlockSpec((GDN_BT, GDN_V_W), lambda i: (i, 0)),
                   pl.BlockSpec((GDN_BT, GDN_HEADS, GDN_K_DIM, GDN_V_DIM), lambda i: (i, 0, 0, 0))],
        out_shape=[jax.ShapeDtypeStruct((db, GDN_V_W), BF16),
                   jax.ShapeDtypeStruct(state.shape, F32)],
        scratch_shapes=[pltpu.VMEM((GDN_BT, GDN_V_W), F32)],
        compiler_params=_cparams("parallel"),
        name="gdn_sample",
    )(gqkv, conv0, gt, c0t, ba, bat, z, state, conv_w, cwt, alog_c, dtb_c, nw)


def _mem_prompt_kernel(q_ref, k_ref, v_ref, o_ref):
    scale = MEM_HEAD_DIM ** -0.5
    for h in range(MEM_HEADS):
        lanes = slice(h * MEM_HEAD_DIM, (h + 1) * MEM_HEAD_DIM)
        s = _dot_nt(q_ref[0, :, lanes], k_ref[0, :, lanes].astype(BF16)) * scale
        m = jnp.max(s, axis=-1, keepdims=True)
        e = jnp.exp(s - m)
        p = e / jnp.sum(e, axis=-1, keepdims=True)
        o_ref[0, :, lanes] = _dot(p.astype(BF16), v_ref[0, :, lanes].astype(BF16)).astype(BF16)


def _mem_prompt(q, mem_k, mem_v, tq):
    b, seq, _ = q.shape
    return pl.pallas_call(
        _mem_prompt_kernel,
        grid=(b, seq // tq),
        in_specs=[pl.BlockSpec((1, tq, MEM_W), lambda i, j: (i, j, 0)),
                  pl.BlockSpec((1, N_MEM, MEM_W), lambda i, j: (i, 0, 0)),
                  pl.BlockSpec((1, N_MEM, MEM_W), lambda i, j: (i, 0, 0))],
        out_specs=pl.BlockSpec((1, tq, MEM_W), lambda i, j: (i, j, 0)),
        out_shape=jax.ShapeDtypeStruct((b, seq, MEM_W), BF16),
        compiler_params=_cparams("parallel", "parallel"),
        name="mem_prompt",
    )(q, mem_k, mem_v)


MEM_BT = 4


def _mem_sample_kernel(q_ref, k_ref, v_ref, o_ref):
    scale = MEM_HEAD_DIM ** -0.5
    rows = 8
    head_of_lane = _iota((rows, MEM_W), 1) // MEM_HEAD_DIM
    own = head_of_lane == _iota((rows, MEM_W), 0)
    for b in range(MEM_BT):
        qm = jnp.where(own, q_ref[b].astype(F32), 0.0).astype(BF16)
        s = _dot_nt(qm, k_ref[b].astype(BF16)) * scale
        m = jnp.max(s, axis=-1, keepdims=True)
        e = jnp.exp(s - m)
        p = e / jnp.sum(e, axis=-1, keepdims=True)
        o = _dot(p.astype(BF16), v_ref[b].astype(BF16))
        o_ref[b] = jnp.sum(jnp.where(own, o, 0.0), axis=0, keepdims=True).astype(BF16)


def _mem_sample(q, mem_k, mem_v):
    db = q.shape[0]
    out = pl.pallas_call(
        _mem_sample_kernel,
        grid=(db // MEM_BT,),
        in_specs=[pl.BlockSpec((MEM_BT, 1, MEM_W), lambda i: (i, 0, 0)),
                  pl.BlockSpec((MEM_BT, N_MEM, MEM_W), lambda i: (i, 0, 0)),
                  pl.BlockSpec((MEM_BT, N_MEM, MEM_W), lambda i: (i, 0, 0))],
        out_specs=pl.BlockSpec((MEM_BT, 1, MEM_W), lambda i: (i, 0, 0)),
        out_shape=jax.ShapeDtypeStruct((db, 1, MEM_W), BF16),
        compiler_params=_cparams("parallel"),
        name="mem_sample",
    )(q.reshape(db, 1, MEM_W), mem_k.reshape(db, N_MEM, MEM_W), mem_v.reshape(db, N_MEM, MEM_W))
    return out.reshape(db, MEM_W)


def _merge_kernel(x_ref, osb_ref, ogdn_ref, omem_ref, wg_ref, wsb_ref, wgdn_ref, wmem_ref, wo_ref, g_ref, b_ref,
                  h_ref):
    x = x_ref[...]
    xb = x.astype(BF16)
    merged = None
    for k, (o_ref, w_ref) in enumerate(((osb_ref, wsb_ref), (ogdn_ref, wgdn_ref), (omem_ref, wmem_ref))):
        gate = _sigmoid(_dot(xb, wg_ref[:, k * D_MODEL:(k + 1) * D_MODEL]))
        term = gate * _dot(o_ref[...], w_ref[...])
        merged = term if merged is None else merged + term
    pre = DN_ALPHA * x + _dot(merged.astype(BF16), wo_ref[...])
    h_ref[...] = _layer_norm(pre, g_ref[...], b_ref[...])


def _merge_ln(x2d, o_sb, o_gdn, o_mem, w_gates, w_sb, w_gdn, w_mem, w_o, g, b, tm):
    t = x2d.shape[0]
    row = lambda w: pl.BlockSpec((tm, w), lambda i: (i, 0))
    return pl.pallas_call(
        _merge_kernel,
        grid=(t // tm,),
        in_specs=[row(D_MODEL), row(512), row(512), row(512),
                  _const_spec((D_MODEL, 3 * D_MODEL)), _const_spec((512, D_MODEL)), _const_spec((512, D_MODEL)),
                  _const_spec((512, D_MODEL)), _const_spec((D_MODEL, D_MODEL)),
                  _const_spec((1, D_MODEL)), _const_spec((1, D_MODEL))],
        out_specs=row(D_MODEL),
        out_shape=jax.ShapeDtypeStruct((t, D_MODEL), F32),
        compiler_params=_cparams("parallel"),
        name="merge_ln",
    )(x2d, o_sb, o_gdn, o_mem, w_gates, w_sb, w_gdn, w_mem, w_o, g, b)


ROUTE_TM = 128
GROUP_SIZE = N_EXPERTS // N_GROUPS


def _first_max(x, idx, n):
    m = jnp.max(x, axis=0, keepdims=True)
    first = jnp.min(jnp.where(x == m, idx, n), axis=0, keepdims=True)
    return m, first


def _route_kernel(h_ref, wr_ref, bias_ref, e_ref, rank_ref, w_ref, cnt_ref, carry):
    i = pl.program_id(0)

    @pl.when(i == 0)
    def _():
        carry[...] = jnp.zeros_like(carry)

    tm = h_ref.shape[0]
    neg = -jnp.inf
    hh, hl = _split2(h_ref[...])
    wh, wl = _split2(wr_ref[...])
    logits = _dot_nt(wh, hh) + _dot_nt(wh, hl) + _dot_nt(wl, hh)
    scores = _sigmoid(logits)
    biased = scores + bias_ref[...]
    grp = []
    sub = _iota((GROUP_SIZE, tm), 0)
    for g in range(N_GROUPS):
        xg = biased[g * GROUP_SIZE:(g + 1) * GROUP_SIZE]
        m1, f1 = _first_max(xg, sub, GROUP_SIZE)
        m2 = jnp.max(jnp.where(sub == f1, neg, xg), axis=0, keepdims=True)
        grp.append(m1 + m2)
    gs = jnp.concatenate(grp, axis=0)
    gidx = _iota((N_GROUPS, tm), 0)
    gsel = gidx < 0
    for _ in range(TOPK_GROUPS):
        _, f = _first_max(gs, gidx, N_GROUPS)
        hit = gidx == f
        gsel = jnp.logical_or(gsel, hit)
        gs = jnp.where(hit, neg, gs)
    eidx = _iota((N_EXPERTS, tm), 0)
    gsel_f = jnp.where(gsel, 1.0, 0.0)
    allowed = jnp.concatenate(
        [jnp.broadcast_to(gsel_f[g:g + 1], (GROUP_SIZE, tm)) for g in range(N_GROUPS)], axis=0) > 0.5
    choice = jnp.where(allowed, biased, neg)
    picks = []
    weights = []
    chosen = jnp.zeros((N_EXPERTS, tm), F32)
    for _ in range(TOP_K):
        _, f = _first_max(choice, eidx, N_EXPERTS)
        hit = eidx == f
        picks.append(f)
        weights.append(jnp.sum(jnp.where(hit, scores, 0.0), axis=0, keepdims=True))
        chosen = jnp.where(hit, 1.0, chosen)
        choice = jnp.where(hit, neg, choice)
    wsum = weights[0]
    for w in weights[1:]:
        wsum = wsum + w
    tr = _iota((tm, tm), 0)
    tc = _iota((tm, tm), 1)
    before = jnp.where(tr < tc, 1.0, 0.0).astype(BF16)
    chosen_b = chosen.astype(BF16)
    rank = _dot(chosen_b, before) + carry[...]
    ranks = [jnp.sum(jnp.where(eidx == f, rank, 0.0), axis=0, keepdims=True) for f in picks]
    e_ref[...] = jnp.concatenate(picks, axis=0)
    rank_ref[...] = jnp.concatenate(ranks, axis=0).astype(I32)
    w_ref[...] = jnp.concatenate(weights, axis=0) / wsum * ROUTE_SCALE
    carry[...] = carry[...] + _dot(chosen_b, jnp.ones((tm, LANES), BF16))
    cnt_ref[...] = carry[...]


def _route(h, w_router_t, bias_col):
    t = h.shape[0]
    tm = ROUTE_TM
    col = lambda: pl.BlockSpec((TOP_K, tm), lambda i: (0, i))
    return pl.pallas_call(
        _route_kernel,
        grid=(t // tm,),
        in_specs=[pl.BlockSpec((tm, D_MODEL), lambda i: (i, 0)),
                  pl.BlockSpec((N_EXPERTS, D_MODEL), lambda i: (0, 0)),
                  pl.BlockSpec((N_EXPERTS, 1), lambda i: (0, 0))],
        out_specs=[col(), col(), col(), pl.BlockSpec((N_EXPERTS, LANES), lambda i: (0, 0))],
        out_shape=[jax.ShapeDtypeStruct((TOP_K, t), I32), jax.ShapeDtypeStruct((TOP_K, t), I32),
                   jax.ShapeDtypeStruct((TOP_K, t), F32), jax.ShapeDtypeStruct((N_EXPERTS, LANES), F32)],
        scratch_shapes=[pltpu.VMEM((N_EXPERTS, LANES), F32)],
        compiler_params=_cparams("arbitrary"),
        name="route",
    )(h, w_router_t, bias_col)


DISPATCH_TM = 128


def _row_copy(src, dst, s_row, d_row, sem):
    return pltpu.make_async_copy(src.at[pl.ds(s_row, 1)], dst.at[pl.ds(d_row, 1)], sem)


def _dispatch_kernel(dest_ref, h_ref, xs_ref, sem):
    base = pl.program_id(0) * DISPATCH_TM

    def issue(t, _):
        for r in range(TOP_K):
            _row_copy(h_ref, xs_ref, base + t, dest_ref[t * TOP_K + r], sem).start()
        return 0

    lax.fori_loop(0, DISPATCH_TM, issue, 0)

    def drain(t, _):
        for r in range(TOP_K):
            _row_copy(h_ref, xs_ref, 0, 0, sem).wait()
        return 0

    lax.fori_loop(0, DISPATCH_TM, drain, 0)


def _dispatch(h, dest_flat):
    t = h.shape[0]
    return pl.pallas_call(
        _dispatch_kernel,
        grid=(t // DISPATCH_TM,),
        in_specs=[pl.BlockSpec((DISPATCH_TM * TOP_K,), lambda i: (i,), memory_space=pltpu.SMEM),
                  pl.BlockSpec(memory_space=pl.ANY)],
        out_specs=pl.BlockSpec(memory_space=pl.ANY),
        out_shape=jax.ShapeDtypeStruct((t * TOP_K, D_MODEL), F32),
        scratch_shapes=[pltpu.SemaphoreType.DMA(())],
        compiler_params=_cparams("arbitrary"),
        name="moe_dispatch",
    )(dest_flat, h)


def _expert_kernel(blk_ref, exp_ref, lo_ref, hi_ref, first_ref, x_ref, wg_ref, wu_ref, wd_ref, y_ref):
    del blk_ref, exp_ref
    i = pl.program_id(0)
    xb = x_ref[...].astype(BF16)
    gate = _dot(xb, wg_ref[0].astype(BF16))
    up = _dot(xb, wu_ref[0].astype(BF16))
    act = gate * _sigmoid(gate) * up
    y = _dot(act.astype(BF16), wd_ref[0].astype(BF16))
    row = _iota((ROW_BLOCK, 1), 0)
    y = jnp.where((row >= lo_ref[i]) & (row < hi_ref[i]), y, 0.0)

    @pl.when(first_ref[i] == 1)
    def _():
        y_ref[...] = y

    @pl.when(first_ref[i] == 0)
    def _():
        y_ref[...] = y_ref[...] + y


def _expert_mm(xs, w_gate, w_up, w_down, blk, exp, lo, hi, first):
    n_items = blk.shape[0]
    grid_spec = pltpu.PrefetchScalarGridSpec(
        num_scalar_prefetch=5,
        grid=(n_items,),
        in_specs=[pl.BlockSpec((ROW_BLOCK, D_MODEL), lambda i, blk, exp, lo, hi, fi: (blk[i], 0)),
                  pl.BlockSpec((1, D_MODEL, D_EXPERT), lambda i, blk, exp, lo, hi, fi: (exp[i], 0, 0)),
                  pl.BlockSpec((1, D_MODEL, D_EXPERT), lambda i, blk, exp, lo, hi, fi: (exp[i], 0, 0)),
                  pl.BlockSpec((1, D_EXPERT, D_MODEL), lambda i, blk, exp, lo, hi, fi: (exp[i], 0, 0))],
        out_specs=pl.BlockSpec((ROW_BLOCK, D_MODEL), lambda i, blk, exp, lo, hi, fi: (blk[i], 0)),
    )
    return pl.pallas_call(
        _expert_kernel,
        grid_spec=grid_spec,
        out_shape=jax.ShapeDtypeStruct(xs.shape, F32),
        compiler_params=_cparams("arbitrary"),
        name="moe_experts",
    )(blk, exp, lo, hi, first, xs, w_gate, w_up, w_down)


COMBINE_TM = 128


def _combine_kernel(dest_ref, h_ref, wt_ref, y_ref, wsg_ref, wsu_ref, wsd_ref, g_ref, b_ref, o_ref, buf, sem):
    def issue(t, _):
        for r in range(TOP_K):
            pltpu.make_async_copy(y_ref.at[pl.ds(dest_ref[t * TOP_K + r], 1)], buf.at[r, pl.ds(t, 1)], sem).start()
        return 0

    lax.fori_loop(0, COMBINE_TM, issue, 0)
    h = h_ref[...]
    hb = h.astype(BF16)
    gate = _dot(hb, wsg_ref[...])
    up = _dot(hb, wsu_ref[...])
    f = _dot((gate * _sigmoid(gate) * up).astype(BF16), wsd_ref[...])

    def drain(t, _):
        for r in range(TOP_K):
            pltpu.make_async_copy(y_ref.at[pl.ds(0, 1)], buf.at[r, pl.ds(t, 1)], sem).wait()
        return 0

    lax.fori_loop(0, COMBINE_TM, drain, 0)
    for r in range(TOP_K):
        f = f + wt_ref[:, r:r + 1] * buf[r]
    o_ref[...] = _layer_norm(DN_ALPHA * h + f, g_ref[...], b_ref[...])


def _combine(h, w_tok, dest_flat, y, w_sg, w_su, w_sd, g, b):
    t = h.shape[0]
    tm = COMBINE_TM
    return pl.pallas_call(
        _combine_kernel,
        grid=(t // tm,),
        in_specs=[pl.BlockSpec((tm * TOP_K,), lambda i: (i,), memory_space=pltpu.SMEM),
                  pl.BlockSpec((tm, D_MODEL), lambda i: (i, 0)),
                  pl.BlockSpec((tm, TOP_K), lambda i: (i, 0)),
                  pl.BlockSpec(memory_space=pl.ANY),
                  _const_spec((D_MODEL, D_SHARED)), _const_spec((D_MODEL, D_SHARED)),
                  _const_spec((D_SHARED, D_MODEL)), _const_spec((1, D_MODEL)), _const_spec((1, D_MODEL))],
        out_specs=pl.BlockSpec((tm, D_MODEL), lambda i: (i, 0)),
        out_shape=jax.ShapeDtypeStruct((t, D_MODEL), F32),
        scratch_shapes=[pltpu.VMEM((TOP_K, tm, D_MODEL), F32), pltpu.SemaphoreType.DMA(())],
        compiler_params=_cparams("arbitrary"),
        name="moe_combine",
    )(dest_flat, h, w_tok, y, w_sg, w_su, w_sd, g, b)


def _moe_plan(counts, n_rows):
    n_blocks = n_rows // ROW_BLOCK
    n_items = n_blocks + N_EXPERTS
    start = jnp.cumsum(counts) - counts
    end = start + counts
    first_blk = start // ROW_BLOCK
    last_blk = jnp.maximum(end - 1, start) // ROW_BLOCK
    nb = jnp.where(counts > 0, last_blk - first_blk + 1, 0)
    item_end = jnp.cumsum(nb)
    item_start = item_end - nb
    total = item_end[-1]
    idx = jnp.arange(n_items, dtype=I32)
    last_item = jnp.maximum(total - 1, 0)
    src = jnp.minimum(idx, last_item)
    exp = jnp.minimum(jnp.searchsorted(item_end, src, side="right"), N_EXPERTS - 1).astype(I32)
    blk = (first_blk[exp] + src - item_start[exp]).astype(I32)
    valid = idx < total
    lo = jnp.clip(start[exp] - blk * ROW_BLOCK, 0, ROW_BLOCK)
    hi = jnp.clip(end[exp] - blk * ROW_BLOCK, 0, ROW_BLOCK)
    first = (valid & (lo == 0)).astype(I32)
    lo = jnp.where(valid, lo, 0).astype(I32)
    hi = jnp.where(valid, hi, 0).astype(I32)
    return start, blk, exp, lo, hi, first


def _moe(h, w_router, router_bias, w_exp_gate, w_exp_up, w_exp_down, w_sg, w_su, w_sd, g, b):
    t = h.shape[0]
    picks, ranks, weights, cnt = _route(h, w_router.T, router_bias.astype(F32).reshape(N_EXPERTS, 1))
    counts = cnt[:, 0].astype(I32)
    start, blk, exp, lo, hi, first = _moe_plan(counts, t * TOP_K)
    dest = (start[picks] + ranks).astype(I32)
    dest_flat = dest.T.reshape(t * TOP_K)
    xs = _dispatch(h, dest_flat)
    y = _expert_mm(xs, w_exp_gate, w_exp_up, w_exp_down, blk, exp, lo, hi, first)
    return _combine(h, weights.T, dest_flat, y, w_sg, w_su, w_sd, g, b)


def kernel(x_prompt, x_sample, cache_sb_k, cache_sb_v, page_table, state_gdn, state_gdn_conv, cache_mem_k, cache_mem_v, mem_prompt, w_in, sb_logit_bias, conv_w, gdn_A_log, gdn_dt_bias, gdn_norm_w, w_mem_k, w_mem_v, w_br_sb, w_br_gdn, w_br_mem, w_o, ln1_g, ln1_b, w_router, router_bias, w_exp_gate, w_exp_up, w_exp_down, w_sh_gate, w_sh_up, w_sh_down, ln2_g, ln2_b):
    bsz, seq, _ = x_prompt.shape
    db = x_sample.shape[0]
    tp = bsz * seq

    ba_off = 3 * SB_W + GDN_CONV_W + GDN_V_W
    mq_off = ba_off + 2 * GDN_HEADS
    gate_off = mq_off + MEM_W
    w_main = jnp.concatenate([w_in[:, :ba_off], w_in[:, mq_off:gate_off]], axis=1).astype(BF16)
    w_ba = jnp.pad(w_in[:, ba_off:mq_off], ((0, 0), (0, LANES - 2 * GDN_HEADS))).astype(BF16)
    w_gates = w_in[:, gate_off:].astype(BF16)
    w_kv = jnp.concatenate([w_mem_k, w_mem_v], axis=1).astype(BF16)
    w_sb, w_gdn, w_mem, w_ob = (w.astype(BF16) for w in (w_br_sb, w_br_gdn, w_br_mem, w_o))
    w_sg, w_su, w_sd = (w.astype(BF16) for w in (w_sh_gate, w_sh_up, w_sh_down))
    row = lambda v: v.astype(F32).reshape(1, -1)
    head_lane = lambda v: jnp.pad(v.astype(F32), (GDN_HEADS, LANES - 2 * GDN_HEADS)).reshape(1, LANES)
    nw_pair = jnp.tile(gdn_norm_w.astype(F32), 2).reshape(1, LANES)
    bias = sb_logit_bias.astype(F32)

    xp = x_prompt.reshape(tp, D_MODEL)
    q, k, v, gqkv, z, mq, ba = _proj_in(xp, w_main, w_ba, 512)
    mem_k, mem_v = _mem_kv(mem_prompt.reshape(bsz * N_MEM, D_MODEL), w_kv, 512)
    sh = lambda a: a.reshape(bsz, seq, a.shape[-1])
    o_sb = _sb_prompt(sh(q), sh(k), sh(v), bias)
    gq3 = sh(gqkv)
    o_gdn, s_p = _gdn_prompt(gq3, sh(z), sh(ba), conv_w.astype(F32), head_lane(gdn_A_log),
                             head_lane(gdn_dt_bias), nw_pair)
    o_mem = _mem_prompt(sh(mq), mem_k.reshape(bsz, N_MEM, MEM_W), mem_v.reshape(bsz, N_MEM, MEM_W), min(512, seq))
    h_p = _merge_ln(xp, o_sb.reshape(tp, SB_W), o_gdn.reshape(tp, GDN_V_W), o_mem.reshape(tp, MEM_W),
                    w_gates, w_sb, w_gdn, w_mem, w_ob, row(ln1_g), row(ln1_b), 512)

    xs = x_sample.reshape(db, D_MODEL)
    q_s, k_s, v_s, gqkv_s, z_s, mq_s, ba_s = _proj_in(xs, w_main, w_ba, db)
    o_sb_s = _sb_sample(q_s, cache_sb_k, cache_sb_v, page_table, bias)
    o_gdn_s, s_s = _gdn_sample(gqkv_s, state_gdn_conv.astype(F32), ba_s, z_s, state_gdn.astype(F32),
                               conv_w.astype(F32), gdn_A_log, gdn_dt_bias, gdn_norm_w)
    o_mem_s = _mem_sample(mq_s, cache_mem_k, cache_mem_v)
    h_s = _merge_ln(xs, o_sb_s, o_gdn_s, o_mem_s, w_gates, w_sb, w_gdn, w_mem, w_ob, row(ln1_g), row(ln1_b), db)

    y = _moe(jnp.concatenate([h_p, h_s], axis=0), w_router, router_bias, w_exp_gate, w_exp_up, w_exp_down,
             w_sg, w_su, w_sd, row(ln2_g), row(ln2_b))

    heads = lambda a, n, hd: a.reshape(n, -1, hd[0], hd[1])
    return (y[:tp].reshape(bsz, seq, D_MODEL), y[tp:].reshape(db, 1, D_MODEL),
            heads(k, bsz, (SB_HEADS, SB_HEAD_DIM)), heads(v, bsz, (SB_HEADS, SB_HEAD_DIM)),
            s_p, gq3[:, seq - (GDN_CONV - 1):, :],
            mem_k.reshape(bsz, N_MEM, MEM_HEADS, MEM_HEAD_DIM), mem_v.reshape(bsz, N_MEM, MEM_HEADS, MEM_HEAD_DIM),
            heads(k_s, db, (SB_HEADS, SB_HEAD_DIM)), heads(v_s, db, (SB_HEADS, SB_HEAD_DIM)),
            s_s, jnp.concatenate([state_gdn_conv[:, 1:, :], gqkv_s[:, None, :]], axis=1))
```

```python
import functools

import jax
import jax.numpy as jnp
from jax import lax
from jax.experimental import pallas as pl
from jax.experimental.pallas import tpu as pltpu

F32 = jnp.float32
BF16 = jnp.bfloat16
I32 = jnp.int32

D_MODEL = 1024
PAGE_SIZE = 128
SB_HEADS = 8
SB_HEAD_DIM = 64
SB_W = SB_HEADS * SB_HEAD_DIM
GDN_HEADS = 8
GDN_K_DIM = 64
GDN_V_DIM = 64
GDN_CONV = 4
GDN_QK_W = GDN_HEADS * GDN_K_DIM
GDN_V_W = GDN_HEADS * GDN_V_DIM
GDN_CONV_W = 2 * GDN_QK_W + GDN_V_W
N_MEM = 256
MEM_HEADS = 4
MEM_HEAD_DIM = 128
MEM_W = MEM_HEADS * MEM_HEAD_DIM
N_EXPERTS = 64
TOP_K = 8
N_GROUPS = 8
TOPK_GROUPS = 4
D_EXPERT = 256
D_SHARED = 256
ROUTE_SCALE = 2.5
DEPTH = 1
DN_ALPHA = (2 * DEPTH) ** 0.25
LN_EPS = 1e-5
RMS_EPS = 1e-6
L2_EPS = 1e-6

LANES = 128
CHUNK = 128
ROW_BLOCK = 256
VMEM_LIMIT = 56 * 1024 * 1024


def _cparams(*sem):
    return pltpu.CompilerParams(dimension_semantics=sem, vmem_limit_bytes=VMEM_LIMIT)


def _dot(a, b):
    return jnp.dot(a, b, preferred_element_type=F32)


def _dot_nt(a, b):
    return lax.dot_general(a, b, (((1,), (1,)), ((), ())), preferred_element_type=F32)


def _split2(x):
    hi = x.astype(BF16)
    lo = (x - hi.astype(F32)).astype(BF16)
    return hi, lo


def _split3(x):
    hi = x.astype(BF16)
    r = x - hi.astype(F32)
    mid = r.astype(BF16)
    lo = (r - mid.astype(F32)).astype(BF16)
    return hi, mid, lo


def _dot_xc(x, m):
    return _dot(jnp.concatenate(_split3(x), axis=1), jnp.concatenate([m, m, m], axis=0))


def _dot_cx(m, x):
    return _dot(jnp.concatenate([m, m, m], axis=1), jnp.concatenate(_split3(x), axis=0))


def _dot_ff(a, b):
    ah, al = _split2(a)
    bh, bl = _split2(b)
    return _dot(jnp.concatenate([ah, ah, al], axis=1), jnp.concatenate([bh, bl, bh], axis=0))


def _sigmoid(x):
    return 1.0 / (1.0 + jnp.exp(-x))


def _softplus(x):
    return jnp.maximum(x, 0.0) + jnp.log(1.0 + jnp.exp(-jnp.abs(x)))


def _iota(shape, dim):
    return lax.broadcasted_iota(I32, shape, dim)


def _const_spec(shape):
    nd = len(shape)
    return pl.BlockSpec(shape, lambda *_: (0,) * nd, pipeline_mode=pl.Buffered(1))


def _layer_norm(x, g, b):
    mu = jnp.mean(x, axis=-1, keepdims=True)
    xc = x - mu
    var = jnp.mean(xc * xc, axis=-1, keepdims=True)
    return xc * lax.rsqrt(var + LN_EPS) * g + b


def _proj_kernel(x_ref, w_ref, wba_ref, q_ref, k_ref, v_ref, g_ref, z_ref, mq_ref, ba_ref):
    xb = x_ref[...].astype(BF16)
    q_ref[...] = (_dot(xb, w_ref[:, 0:512]) * 0.125).astype(BF16)
    k_ref[...] = _dot(xb, w_ref[:, 512:1024])
    v_ref[...] = _dot(xb, w_ref[:, 1024:1536])
    for c in range(3):
        g_ref[:, c * 512:(c + 1) * 512] = _dot(xb, w_ref[:, 1536 + c * 512:2048 + c * 512])
    z_ref[...] = _dot(xb, w_ref[:, 3072:3584]).astype(BF16)
    mq_ref[...] = _dot(xb, w_ref[:, 3584:4096]).astype(BF16)
    ba_ref[...] = _dot(xb, wba_ref[...])


def _proj_in(x2d, w_main, w_ba, tm):
    t = x2d.shape[0]
    row = lambda w: pl.BlockSpec((tm, w), lambda i: (i, 0))
    return pl.pallas_call(
        _proj_kernel,
        grid=(t // tm,),
        in_specs=[row(D_MODEL), _const_spec((D_MODEL, 4096)), _const_spec((D_MODEL, LANES))],
        out_specs=[row(512), row(512), row(512), row(1536), row(512), row(512), row(LANES)],
        out_shape=[jax.ShapeDtypeStruct((t, 512), BF16), jax.ShapeDtypeStruct((t, 512), F32),
                   jax.ShapeDtypeStruct((t, 512), F32), jax.ShapeDtypeStruct((t, 1536), F32),
                   jax.ShapeDtypeStruct((t, 512), BF16), jax.ShapeDtypeStruct((t, 512), BF16),
                   jax.ShapeDtypeStruct((t, LANES), F32)],
        compiler_params=_cparams("parallel"),
        name="proj_in",
    )(x2d, w_main, w_ba)


def _memkv_kernel(x_ref, w_ref, k_ref, v_ref):
    xb = x_ref[...].astype(BF16)
    k_ref[...] = _dot(xb, w_ref[:, 0:512])
    v_ref[...] = _dot(xb, w_ref[:, 512:1024])


def _mem_kv(x2d, w_kv, tm):
    t = x2d.shape[0]
    row = lambda w: pl.BlockSpec((tm, w), lambda i: (i, 0))
    return pl.pallas_call(
        _memkv_kernel,
        grid=(t // tm,),
        in_specs=[row(D_MODEL), _const_spec((D_MODEL, 1024))],
        out_specs=[row(512), row(512)],
        out_shape=[jax.ShapeDtypeStruct((t, 512), F32)] * 2,
        compiler_params=_cparams("parallel"),
        name="mem_kv",
    )(x2d, w_kv)


def _cumsum_matrix():
    r = _iota((2 * CHUNK, 2 * CHUNK), 0) % CHUNK
    c = _iota((2 * CHUNK, 2 * CHUNK), 1)
    return jnp.where((c >= CHUNK) | (r > c), 1.0, 0.0).astype(BF16)


SB_QT = 512


def _sb_prompt_kernel(qt, bias_ref, q_ref, k_ref, v_ref, o_ref, qs, kbs, vs, carry, acc):
    hp = pl.program_id(1)
    seq = q_ref.shape[1]
    nblk = seq // CHUNK
    first = _iota((CHUNK, LANES), 1) < SB_HEAD_DIM
    tt = _cumsum_matrix()
    dist = _iota((qt, LANES), 0) - _iota((qt, LANES), 1)
    biases = (bias_ref[2 * hp], bias_ref[2 * hp + 1])

    def stage(i, _):
        rows = pl.ds(pl.multiple_of(i * CHUNK, CHUNK), CHUNK)
        qf = q_ref[0, rows, :].astype(F32)
        qs[0, rows, :] = jnp.where(first, qf, 0.0).astype(BF16)
        qs[1, rows, :] = jnp.where(first, 0.0, qf).astype(BF16)
        kbs[rows, :] = k_ref[0, rows, :].astype(BF16)
        vf = v_ref[0, rows, :]
        base = pl.multiple_of(i * 2 * CHUNK, 2 * CHUNK)
        vs[pl.ds(base, CHUNK), :] = jnp.where(first, vf, 0.0).astype(BF16)
        vs[pl.ds(base + CHUNK, CHUNK), :] = jnp.where(first, 0.0, vf).astype(BF16)
        zero = jnp.zeros((CHUNK, LANES), F32)
        carry[0, rows, :] = zero
        carry[1, rows, :] = zero
        acc[rows, :] = zero
        return 0

    lax.fori_loop(0, nblk, stage, 0)

    def kstep(jj, _):
        j = nblk - 1 - jj
        key0 = j * CHUNK
        kj = kbs[pl.ds(pl.multiple_of(key0, CHUNK), CHUNK), :]
        vj = vs[pl.ds(pl.multiple_of(j * 2 * CHUNK, 2 * CHUNK), 2 * CHUNK), :]

        def qtile(t, _):
            row0 = pl.multiple_of(t * qt, qt)
            rows = pl.ds(row0, qt)
            causal = dist > key0 - row0
            z2 = _dot_nt(jnp.concatenate([qs[0, rows, :], qs[1, rows, :]], axis=0), kj)
            zs, sps, parts = [], [], []
            for h in range(2):
                z = z2[h * qt:(h + 1) * qt] + biases[h]
                sp = _softplus(z)
                hi, lo = _split2(jnp.where(causal, sp, 0.0))
                zs.append(z)
                sps.append(sp)
                parts.append(jnp.concatenate([hi, lo], axis=1))
            r2 = _dot(jnp.concatenate(parts, axis=0), tt)
            a = []
            for h in range(2):
                r = r2[h * qt:(h + 1) * qt]
                c = carry[h, rows, :]
                e = jnp.exp(zs[h] - sps[h] - r[:, :CHUNK] - c)
                a.append(jnp.where(causal, e, 0.0).astype(BF16))
                carry[h, rows, :] = c + r[:, CHUNK:]
            acc[rows, :] = acc[rows, :] + _dot(jnp.concatenate(a, axis=1), vj)
            return 0

        lax.fori_loop(key0 // qt, seq // qt, qtile, 0)
        return 0

    lax.fori_loop(0, nblk, kstep, 0)

    def finish(i, _):
        rows = pl.ds(pl.multiple_of(i * CHUNK, CHUNK), CHUNK)
        o_ref[0, rows, :] = acc[rows, :].astype(BF16)
        return 0

    lax.fori_loop(0, nblk, finish, 0)


def _sb_prompt(q, k, v, bias):
    b, seq, _ = q.shape
    qt = min(SB_QT, seq)
    blk = lambda: pl.BlockSpec((1, seq, LANES), lambda i, j: (i, 0, j))
    return pl.pallas_call(
        functools.partial(_sb_prompt_kernel, qt),
        grid=(b, SB_W // LANES),
        in_specs=[pl.BlockSpec(memory_space=pltpu.SMEM), blk(), blk(), blk()],
        out_specs=blk(),
        out_shape=jax.ShapeDtypeStruct((b, seq, SB_W), BF16),
        scratch_shapes=[pltpu.VMEM((2, seq, LANES), BF16), pltpu.VMEM((seq, LANES), BF16),
                        pltpu.VMEM((2 * seq, LANES), BF16), pltpu.VMEM((2, seq, LANES), F32),
                        pltpu.VMEM((seq, LANES), F32)],
        compiler_params=_cparams("parallel", "parallel"),
        name="sb_prompt",
    )(bias, q, k, v)


def _sb_sample_kernel(n_pages, pt_ref, q_ref, bias_ref, *refs):
    del pt_ref
    k_refs = refs[:n_pages]
    v_refs = refs[n_pages:2 * n_pages]
    o_ref = refs[2 * n_pages]
    head_of_lane = _iota((SB_HEADS, SB_W), 1) // SB_HEAD_DIM
    own = head_of_lane == _iota((SB_HEADS, SB_W), 0)
    qm = jnp.where(own, q_ref[0].astype(F32), 0.0).astype(BF16)
    pages = range(n_pages)
    kcat = jnp.concatenate([k_refs[p][0].astype(BF16) for p in pages], axis=1)
    z = _dot(qm, kcat)
    zr = jnp.concatenate([z[:, p * PAGE_SIZE:(p + 1) * PAGE_SIZE] + bias_ref[...] for p in pages], axis=0)
    sp = _softplus(zr)
    hi, lo = _split2(sp)
    r = _dot(jnp.concatenate([hi, lo], axis=1), _cumsum_matrix())
    carries = [None] * n_pages
    c = jnp.zeros((SB_HEADS, LANES), F32)
    for p in reversed(pages):
        carries[p] = c
        c = c + r[p * SB_HEADS:(p + 1) * SB_HEADS, PAGE_SIZE:]
    a = jnp.exp(zr - sp - r[:, :PAGE_SIZE] - jnp.concatenate(carries, axis=0))
    acat = jnp.concatenate([a[p * SB_HEADS:(p + 1) * SB_HEADS] for p in pages], axis=1).astype(BF16)
    vcat = jnp.concatenate([v_refs[p][0].astype(BF16) for p in pages], axis=1)
    o = _dot_nt(acat, vcat)
    o_ref[0] = jnp.sum(jnp.where(own, o, 0.0), axis=0, keepdims=True).astype(BF16)


def _sb_sample(q, cache_k, cache_v, page_table, bias):
    db, n_pages = page_table.shape
    n_pool = cache_k.shape[0]
    ck = cache_k.transpose(0, 2, 3, 1).reshape(n_pool, SB_W, PAGE_SIZE)
    cv = cache_v.transpose(0, 2, 3, 1).reshape(n_pool, SB_W, PAGE_SIZE)
    bias2d = jnp.broadcast_to(bias.astype(F32)[:, None], (SB_HEADS, LANES))

    def page_spec(p):
        return pl.BlockSpec((1, SB_W, PAGE_SIZE), lambda i, pt: (pt[i, p], 0, 0))

    grid_spec = pltpu.PrefetchScalarGridSpec(
        num_scalar_prefetch=1,
        grid=(db,),
        in_specs=[pl.BlockSpec((1, 1, SB_W), lambda i, pt: (i, 0, 0)),
                  pl.BlockSpec((SB_HEADS, LANES), lambda i, pt: (0, 0))]
        + [page_spec(p) for p in range(n_pages)] * 2,
        out_specs=pl.BlockSpec((1, 1, SB_W), lambda i, pt: (i, 0, 0)),
    )
    out = pl.pallas_call(
        functools.partial(_sb_sample_kernel, n_pages),
        grid_spec=grid_spec,
        out_shape=jax.ShapeDtypeStruct((db, 1, SB_W), BF16),
        compiler_params=_cparams("parallel"),
        name="sb_sample",
    )(page_table, q.reshape(db, 1, SB_W), bias2d, *([ck] * n_pages), *([cv] * n_pages))
    return out.reshape(db, SB_W)


def _head_ones():
    r = _iota((LANES, LANES), 0) // GDN_K_DIM
    c = _iota((LANES, LANES), 1) // GDN_K_DIM
    return jnp.where(r == c, 1.0, 0.0).astype(BF16)


def _unit_lower_solve(a, rhs):
    p = -a
    y = rhs
    for _ in range(6):
        r = _dot_ff(p, jnp.concatenate([p, y], axis=1))
        y = y + r[:, CHUNK:]
        p = r[:, :CHUNK]
    return y + _dot_ff(p, y)


GDN_TILE = 512
GDN_GROUP = 4


def _gdn_prompt_kernel(tile, group, qp_ref, kp_ref, vp_ref, z_ref, ba_ref, cwq_ref, cwk_ref, cwv_ref, alog_ref, dtb_ref,
                       nw_ref, o_ref, s_ref, qs, ks, vs, exs, os_):
    hp = pl.program_id(1)
    seq = qp_ref.shape[1]
    pad = 8
    for scr, src in ((qs, qp_ref), (ks, kp_ref), (vs, vp_ref)):
        scr[0:pad, :] = jnp.zeros((pad, LANES), F32)
        scr[pad:seq + pad, :] = src[0]

    row = _iota((CHUNK, CHUNK), 0)
    col = _iota((CHUNK, CHUNK), 1)
    lower = row >= col
    strict = row > col
    first = col < GDN_K_DIM
    same_head = (row // GDN_K_DIM) == (col // GDN_K_DIM)
    head_ones = _head_ones()
    lower_ones = jnp.where(lower, 1.0, 0.0).astype(BF16)
    er = _iota((LANES, 4 * LANES), 0)
    ec = _iota((LANES, 4 * LANES), 1)
    sec = ec // LANES
    pair_head = 2 * hp + (ec % LANES) // GDN_K_DIM
    want = jnp.where(sec == 0, pair_head,
                     jnp.where(sec == 1, GDN_HEADS + pair_head,
                               jnp.where(sec == 2, GDN_HEADS + 2 * hp, GDN_HEADS + 2 * hp + 1)))
    expand = jnp.where(er == want, 1.0, 0.0).astype(BF16)
    lane_row = _iota((1, LANES), 1)
    neg_a = -jnp.exp(alog_ref[...])
    scale = GDN_K_DIM ** -0.5

    n_sub = tile // CHUNK

    def conv(scr, cw_ref, r0):
        w = scr[pl.ds(r0, tile + pad), :]
        y = (w[5:5 + tile] * cw_ref[0:1, :] + w[6:6 + tile] * cw_ref[1:2, :]
             + w[7:7 + tile] * cw_ref[2:3, :] + w[8:8 + tile] * cw_ref[3:4, :])
        return y * _sigmoid(y)

    def prepare(tt, _):
        r0 = pl.multiple_of((seq // tile - 1 - tt) * tile, tile)
        rows = pl.ds(r0, tile)
        q = conv(qs, cwq_ref, r0)
        k = conv(ks, cwk_ref, r0)
        v = conv(vs, cwv_ref, r0)
        ss = _dot_xc(jnp.concatenate([q * q, k * k], axis=0), head_ones)
        qs[pl.ds(r0 + pad, tile), :] = q * lax.rsqrt(ss[:tile] + L2_EPS) * scale
        ks[pl.ds(r0 + pad, tile), :] = k * lax.rsqrt(ss[tile:] + L2_EPS)
        vs[pl.ds(r0 + pad, tile), :] = v
        raw = ba_ref[0, rows, :]
        beta_raw = _sigmoid(raw)
        g_raw = neg_a * _softplus(raw + dtb_ref[...])
        g_wide = jnp.concatenate([g_raw[i * CHUNK:(i + 1) * CHUNK] for i in range(n_sub)], axis=1)
        c_wide = _dot_cx(lower_ones, g_wide)
        gcum = jnp.concatenate([c_wide[:, i * LANES:(i + 1) * LANES] for i in range(n_sub)], axis=0)
        exs[rows, :] = _dot_xc(jnp.where(lane_row < GDN_HEADS, beta_raw, gcum), expand)
        return 0

    lax.fori_loop(0, seq // tile, prepare, 0)

    def within_chunk(c):
        r0 = pl.multiple_of(c * CHUNK, CHUNK)
        rows = pl.ds(r0, CHUNK)
        srows = pl.ds(r0 + pad, CHUNK)
        qn = qs[srows, :]
        kn = ks[srows, :]
        v = vs[srows, :]
        ex = exs[rows, :]
        beta_p = ex[:, 0:LANES]
        gc_p = ex[:, LANES:2 * LANES]
        eg_p = jnp.exp(gc_p)
        g_last = gc_p[CHUNK - 1:CHUNK, :]
        kb = kn * beta_p
        lhs = jnp.concatenate([jnp.where(first, kb, 0.0), jnp.where(first, qn, 0.0),
                               jnp.where(first, 0.0, kb), jnp.where(first, 0.0, qn)], axis=0).astype(BF16)
        kq = _dot_nt(lhs, kn.astype(BF16))
        rhs = jnp.concatenate([v * beta_p, kb * eg_p], axis=1)
        sols = []
        qks = []
        for h in range(2):
            g1 = ex[:, (2 + h) * LANES:(3 + h) * LANES]
            diff = g1 - g1.T
            dec = jnp.where(lower, jnp.exp(jnp.where(lower, diff, 0.0)), 0.0)
            kk = kq[2 * h * CHUNK:(2 * h + 1) * CHUNK]
            sols.append(_unit_lower_solve(jnp.where(strict, kk * dec, 0.0), rhs))
            qks.append((kq[(2 * h + 1) * CHUNK:(2 * h + 2) * CHUNK] * dec).astype(BF16))
        first2 = jnp.concatenate([first, first], axis=1)
        sol = jnp.where(first2, sols[0], sols[1])
        u = sol[:, 0:LANES]
        wq = jnp.concatenate([sol[:, LANES:2 * LANES], qn * eg_p], axis=0).astype(BF16)
        k_tail_t = (kn * jnp.exp(g_last - gc_p)).T.astype(BF16)
        return rows, u, wq, jnp.concatenate(qks, axis=1), k_tail_t, jnp.exp(g_last)

    def advance(s, parts):
        rows, u, wq, qk2, k_tail_t, decay = parts
        ws = _dot(wq, s.astype(BF16))
        v_new = u - ws[:CHUNK]
        v_heads = jnp.concatenate([jnp.where(first, v_new, 0.0), jnp.where(first, 0.0, v_new)], axis=0)
        os_[rows, :] = ws[CHUNK:] + _dot(qk2, v_heads.astype(BF16))
        upd = _dot(k_tail_t, v_new.astype(BF16))
        return s * decay + jnp.where(same_head, upd, 0.0)

    def chunk_group(cg, s):
        parts = [within_chunk(cg * group + i) for i in range(group)]
        for p in parts:
            s = advance(s, p)
        return s

    s = lax.fori_loop(0, seq // (CHUNK * group), chunk_group, jnp.zeros((LANES, LANES), F32))
    s_ref[0, 0] = s[0:GDN_K_DIM, 0:GDN_V_DIM]
    s_ref[0, 1] = s[GDN_K_DIM:, GDN_V_DIM:]

    def finish(tt, _):
        rows = pl.ds(pl.multiple_of(tt * tile, tile), tile)
        o = os_[rows, :]
        ms = _dot_xc(o * o, head_ones) * (1.0 / GDN_V_DIM)
        zf = z_ref[0, rows, :].astype(F32)
        o_ref[0, rows, :] = (o * lax.rsqrt(ms + RMS_EPS) * nw_ref[...] * (zf * _sigmoid(zf))).astype(BF16)
        return 0

    lax.fori_loop(0, seq // tile, finish, 0)


def _gdn_prompt(gqkv, z, ba, conv_w, alog_lane, dtb_lane, nw_lane):
    b, seq, _ = gqkv.shape
    n_pair = GDN_HEADS // 2
    blk = lambda off: pl.BlockSpec((1, seq, LANES), lambda i, j: (i, 0, off + j))
    cw = lambda off: pl.BlockSpec((GDN_CONV, LANES), lambda i, j: (0, off + j))
    lane_vec = pl.BlockSpec((1, LANES), lambda i, j: (0, 0))
    return pl.pallas_call(
        functools.partial(_gdn_prompt_kernel, min(GDN_TILE, seq), min(GDN_GROUP, seq // CHUNK)),
        grid=(b, n_pair),
        in_specs=[blk(0), blk(n_pair), blk(2 * n_pair), blk(0),
                  pl.BlockSpec((1, seq, LANES), lambda i, j: (i, 0, 0)),
                  cw(0), cw(n_pair), cw(2 * n_pair), lane_vec, lane_vec,
                  pl.BlockSpec((1, LANES), lambda i, j: (0, 0))],
        out_specs=[blk(0), pl.BlockSpec((1, 2, GDN_K_DIM, GDN_V_DIM), lambda i, j: (i, j, 0, 0))],
        out_shape=[jax.ShapeDtypeStruct((b, seq, GDN_V_W), BF16),
                   jax.ShapeDtypeStruct((b, GDN_HEADS, GDN_K_DIM, GDN_V_DIM), F32)],
        scratch_shapes=[pltpu.VMEM((seq + 8, LANES), F32)] * 3
        + [pltpu.VMEM((seq, 4 * LANES), F32), pltpu.VMEM((seq, LANES), F32)],
        compiler_params=_cparams("parallel", "parallel"),
        name="gdn_prompt",
    )(gqkv, gqkv, gqkv, z, ba, conv_w, conv_w, conv_w, alog_lane, dtb_lane, nw_lane)


GDN_BT = 8


def _gdn_sample_kernel(g_ref, c0_ref, gt_ref, c0t_ref, ba_ref, bat_ref, z_ref, s_ref, cw_ref, cwt_ref,
                       alogc_ref, dtbc_ref, nw_ref, o_ref, sn_ref, o_scr):
    scale = GDN_K_DIM ** -0.5
    y = (c0_ref[:, 0, :] * cw_ref[0:1, :] + c0_ref[:, 1, :] * cw_ref[1:2, :]
         + c0_ref[:, 2, :] * cw_ref[2:3, :] + g_ref[...] * cw_ref[3:4, :])
    v_row = (y * _sigmoid(y))[:, 2 * GDN_QK_W:]
    raw = ba_ref[...]
    er = _iota((LANES, GDN_V_W), 0)
    ec = _iota((LANES, GDN_V_W), 1)
    expand = jnp.where(er == ec // GDN_V_DIM, 1.0, 0.0).astype(BF16)
    beta_row = _dot_xc(_sigmoid(raw), expand)
    u_row = v_row * beta_row
    yt = (c0t_ref[0, 0] * cwt_ref[0] + c0t_ref[0, 1] * cwt_ref[1] + c0t_ref[0, 2] * cwt_ref[2]
          + gt_ref[0] * cwt_ref[3])
    yt = yt * _sigmoid(yt)
    rawt = bat_ref[0]
    beta_c = _sigmoid(rawt[0:GDN_HEADS])
    g_c = -jnp.exp(alogc_ref[...]) * _softplus(rawt[GDN_HEADS:] + dtbc_ref[...])
    eg_c = jnp.exp(g_c)
    for h in range(GDN_HEADS):
        qh = yt[h * GDN_K_DIM:(h + 1) * GDN_K_DIM]
        kh = yt[GDN_QK_W + h * GDN_K_DIM:GDN_QK_W + (h + 1) * GDN_K_DIM]
        qn = qh * lax.rsqrt(jnp.sum(qh * qh, axis=0, keepdims=True) + L2_EPS) * scale
        kn = kh * lax.rsqrt(jnp.sum(kh * kh, axis=0, keepdims=True) + L2_EPS)
        eg = eg_c[h:h + 1, :]
        wn = kn * (beta_c[h:h + 1, :] * eg)
        qd = qn * eg
        qk = jnp.sum(qn * kn, axis=0, keepdims=True)
        for b in range(GDN_BT):
            s = s_ref[b, h]
            u = u_row[b:b + 1, h * GDN_V_DIM:(h + 1) * GDN_V_DIM]
            v_new = u - jnp.sum(wn[:, b:b + 1] * s, axis=0, keepdims=True)
            o = jnp.sum(qd[:, b:b + 1] * s, axis=0, keepdims=True) + qk[:, b:b + 1] * v_new
            sn_ref[b, h] = s * eg[:, b:b + 1] + kn[:, b:b + 1] * v_new
            o_scr[b:b + 1, h * GDN_V_DIM:(h + 1) * GDN_V_DIM] = o
    o = o_scr[...]
    r = _iota((GDN_V_W, GDN_V_W), 0) // GDN_V_DIM
    c = _iota((GDN_V_W, GDN_V_W), 1) // GDN_V_DIM
    ms = _dot_xc(o * o, jnp.where(r == c, 1.0, 0.0).astype(BF16)) * (1.0 / GDN_V_DIM)
    zf = z_ref[...].astype(F32)
    o_ref[...] = (o * lax.rsqrt(ms + RMS_EPS) * nw_ref[...] * (zf * _sigmoid(zf))).astype(BF16)


def _gdn_sample(gqkv, conv0, ba, z, state, conv_w, alog, dtb, norm_w):
    db = gqkv.shape[0]
    nt = db // GDN_BT
    qk_w = 2 * GDN_QK_W
    gt = gqkv[:, :qk_w].T.reshape(qk_w, nt, GDN_BT).transpose(1, 0, 2)
    c0t = conv0[:, :, :qk_w].transpose(1, 2, 0).reshape(3, qk_w, nt, GDN_BT).transpose(2, 0, 1, 3)
    bat = ba[:, :2 * GDN_HEADS].T.reshape(2 * GDN_HEADS, nt, GDN_BT).transpose(1, 0, 2)
    cwt = conv_w[:, :qk_w].reshape(GDN_CONV, qk_w, 1)
    alog_c = alog.astype(F32).reshape(GDN_HEADS, 1)
    dtb_c = dtb.astype(F32).reshape(GDN_HEADS, 1)
    nw = jnp.tile(norm_w.astype(F32), GDN_HEADS).reshape(1, GDN_V_W)
    full = lambda shape: pl.BlockSpec(shape, lambda i: (0,) * len(shape))
    return pl.pallas_call(
        _gdn_sample_kernel,
        grid=(nt,),
        in_specs=[pl.BlockSpec((GDN_BT, GDN_CONV_W), lambda i: (i, 0)),
                  pl.BlockSpec((GDN_BT, GDN_CONV - 1, GDN_CONV_W), lambda i: (i, 0, 0)),
                  pl.BlockSpec((1, qk_w, GDN_BT), lambda i: (i, 0, 0)),
                  pl.BlockSpec((1, GDN_CONV - 1, qk_w, GDN_BT), lambda i: (i, 0, 0, 0)),
                  pl.BlockSpec((GDN_BT, LANES), lambda i: (i, 0)),
                  pl.BlockSpec((1, 2 * GDN_HEADS, GDN_BT), lambda i: (i, 0, 0)),
                  pl.BlockSpec((GDN_BT, GDN_V_W), lambda i: (i, 0)),
                  pl.BlockSpec((GDN_BT, GDN_HEADS, GDN_K_DIM, GDN_V_DIM), lambda i: (i, 0, 0, 0)),
                  full((GDN_CONV, GDN_CONV_W)), full((GDN_CONV, qk_w, 1)),
                  full((GDN_HEADS, 1)), full((GDN_HEADS, 1)), full((1, GDN_V_W))],
        out_specs=[pl.BlockSpec((GDN_BT, GDN_V_W), lambda i: (i, 0)),
                   pl.BlockSpec((GDN_BT, GDN_HEADS, GDN_K_DIM, GDN_V_DIM), lambda i: (i, 0, 0, 0))],
        out_shape=[jax.ShapeDtypeStruct((db, GDN_V_W), BF16),
                   jax.ShapeDtypeStruct(state.shape, F32)],
        scratch_shapes=[pltpu.VMEM((GDN_BT, GDN_V_W), F32)],
        compiler_params=_cparams("parallel"),
        name="gdn_sample",
    )(gqkv, conv0, gt, c0t, ba, bat, z, state, conv_w, cwt, alog_c, dtb_c, nw)


def _mem_prompt_kernel(q_ref, k_ref, v_ref, o_ref):
    scale = MEM_HEAD_DIM ** -0.5
    for h in range(MEM_HEADS):
        lanes = slice(h * MEM_HEAD_DIM, (h + 1) * MEM_HEAD_DIM)
        s = _dot_nt(q_ref[0, :, lanes], k_ref[0, :, lanes].astype(BF16)) * scale
        m = jnp.max(s, axis=-1, keepdims=True)
        e = jnp.exp(s - m)
        p = e / jnp.sum(e, axis=-1, keepdims=True)
        o_ref[0, :, lanes] = _dot(p.astype(BF16), v_ref[0, :, lanes].astype(BF16)).astype(BF16)


def _mem_prompt(q, mem_k, mem_v, tq):
    b, seq, _ = q.shape
    return pl.pallas_call(
        _mem_prompt_kernel,
        grid=(b, seq // tq),
        in_specs=[pl.BlockSpec((1, tq, MEM_W), lambda i, j: (i, j, 0)),
                  pl.BlockSpec((1, N_MEM, MEM_W), lambda i, j: (i, 0, 0)),
                  pl.BlockSpec((1, N_MEM, MEM_W), lambda i, j: (i, 0, 0))],
        out_specs=pl.BlockSpec((1, tq, MEM_W), lambda i, j: (i, j, 0)),
        out_shape=jax.ShapeDtypeStruct((b, seq, MEM_W), BF16),
        compiler_params=_cparams("parallel", "parallel"),
        name="mem_prompt",
    )(q, mem_k, mem_v)


MEM_BT = 4


def _mem_sample_kernel(q_ref, k_ref, v_ref, o_ref):
    scale = MEM_HEAD_DIM ** -0.5
    rows = 8
    head_of_lane = _iota((rows, MEM_W), 1) // MEM_HEAD_DIM
    own = head_of_lane == _iota((rows, MEM_W), 0)
    for b in range(MEM_BT):
        qm = jnp.where(own, q_ref[b].astype(F32), 0.0).astype(BF16)
        s = _dot_nt(qm, k_ref[b].astype(BF16)) * scale
        m = jnp.max(s, axis=-1, keepdims=True)
        e = jnp.exp(s - m)
        p = e / jnp.sum(e, axis=-1, keepdims=True)
        o = _dot(p.astype(BF16), v_ref[b].astype(BF16))
        o_ref[b] = jnp.sum(jnp.where(own, o, 0.0), axis=0, keepdims=True).astype(BF16)


def _mem_sample(q, mem_k, mem_v):
    db = q.shape[0]
    out = pl.pallas_call(
        _mem_sample_kernel,
        grid=(db // MEM_BT,),
        in_specs=[pl.BlockSpec((MEM_BT, 1, MEM_W), lambda i: (i, 0, 0)),
                  pl.BlockSpec((MEM_BT, N_MEM, MEM_W), lambda i: (i, 0, 0)),
                  pl.BlockSpec((MEM_BT, N_MEM, MEM_W), lambda i: (i, 0, 0))],
        out_specs=pl.BlockSpec((MEM_BT, 1, MEM_W), lambda i: (i, 0, 0)),
        out_shape=jax.ShapeDtypeStruct((db, 1, MEM_W), BF16),
        compiler_params=_cparams("parallel"),
        name="mem_sample",
    )(q.reshape(db, 1, MEM_W), mem_k.reshape(db, N_MEM, MEM_W), mem_v.reshape(db, N_MEM, MEM_W))
    return out.reshape(db, MEM_W)


def _merge_kernel(x_ref, osb_ref, ogdn_ref, omem_ref, wg_ref, wsb_ref, wgdn_ref, wmem_ref, wo_ref, g_ref, b_ref,
                  h_ref):
    x = x_ref[...]
    xb = x.astype(BF16)
    merged = None
    for k, (o_ref, w_ref) in enumerate(((osb_ref, wsb_ref), (ogdn_ref, wgdn_ref), (omem_ref, wmem_ref))):
        gate = _sigmoid(_dot(xb, wg_ref[:, k * D_MODEL:(k + 1) * D_MODEL]))
        term = gate * _dot(o_ref[...], w_ref[...])
        merged = term if merged is None else merged + term
    pre = DN_ALPHA * x + _dot(merged.astype(BF16), wo_ref[...])
    h_ref[...] = _layer_norm(pre, g_ref[...], b_ref[...])


def _merge_ln(x2d, o_sb, o_gdn, o_mem, w_gates, w_sb, w_gdn, w_mem, w_o, g, b, tm):
    t = x2d.shape[0]
    row = lambda w: pl.BlockSpec((tm, w), lambda i: (i, 0))
    return pl.pallas_call(
        _merge_kernel,
        grid=(t // tm,),
        in_specs=[row(D_MODEL), row(512), row(512), row(512),
                  _const_spec((D_MODEL, 3 * D_MODEL)), _const_spec((512, D_MODEL)), _const_spec((512, D_MODEL)),
                  _const_spec((512, D_MODEL)), _const_spec((D_MODEL, D_MODEL)),
                  _const_spec((1, D_MODEL)), _const_spec((1, D_MODEL))],
        out_specs=row(D_MODEL),
        out_shape=jax.ShapeDtypeStruct((t, D_MODEL), F32),
        compiler_params=_cparams("parallel"),
        name="merge_ln",
    )(x2d, o_sb, o_gdn, o_mem, w_gates, w_sb, w_gdn, w_mem, w_o, g, b)


ROUTE_TM = 128
GROUP_SIZE = N_EXPERTS // N_GROUPS


def _first_max(x, idx, n):
    m = jnp.max(x, axis=0, keepdims=True)
    first = jnp.min(jnp.where(x == m, idx, n), axis=0, keepdims=True)
    return m, first


def _route_kernel(h_ref, wr_ref, bias_ref, e_ref, rank_ref, w_ref, cnt_ref, carry):
    i = pl.program_id(0)

    @pl.when(i == 0)
    def _():
        carry[...] = jnp.zeros_like(carry)

    tm = h_ref.shape[0]
    neg = -jnp.inf
    hh, hl = _split2(h_ref[...])
    wh, wl = _split2(wr_ref[...])
    logits = _dot_nt(wh, hh) + _dot_nt(wh, hl) + _dot_nt(wl, hh)
    scores = _sigmoid(logits)
    biased = scores + bias_ref[...]
    grp = []
    sub = _iota((GROUP_SIZE, tm), 0)
    for g in range(N_GROUPS):
        xg = biased[g * GROUP_SIZE:(g + 1) * GROUP_SIZE]
        m1, f1 = _first_max(xg, sub, GROUP_SIZE)
        m2 = jnp.max(jnp.where(sub == f1, neg, xg), axis=0, keepdims=True)
        grp.append(m1 + m2)
    gs = jnp.concatenate(grp, axis=0)
    gidx = _iota((N_GROUPS, tm), 0)
    gsel = gidx < 0
    for _ in range(TOPK_GROUPS):
        _, f = _first_max(gs, gidx, N_GROUPS)
        hit = gidx == f
        gsel = jnp.logical_or(gsel, hit)
        gs = jnp.where(hit, neg, gs)
    eidx = _iota((N_EXPERTS, tm), 0)
    gsel_f = jnp.where(gsel, 1.0, 0.0)
    allowed = jnp.concatenate(
        [jnp.broadcast_to(gsel_f[g:g + 1], (GROUP_SIZE, tm)) for g in range(N_GROUPS)], axis=0) > 0.5
    choice = jnp.where(allowed, biased, neg)
    picks = []
    weights = []
    chosen = jnp.zeros((N_EXPERTS, tm), F32)
    for _ in range(TOP_K):
        _, f = _first_max(choice, eidx, N_EXPERTS)
        hit = eidx == f
        picks.append(f)
        weights.append(jnp.sum(jnp.where(hit, scores, 0.0), axis=0, keepdims=True))
        chosen = jnp.where(hit, 1.0, chosen)
        choice = jnp.where(hit, neg, choice)
    wsum = weights[0]
    for w in weights[1:]:
        wsum = wsum + w
    tr = _iota((tm, tm), 0)
    tc = _iota((tm, tm), 1)
    before = jnp.where(tr < tc, 1.0, 0.0).astype(BF16)
    chosen_b = chosen.astype(BF16)
    rank = _dot(chosen_b, before) + carry[...]
    ranks = [jnp.sum(jnp.where(eidx == f, rank, 0.0), axis=0, keepdims=True) for f in picks]
    e_ref[...] = jnp.concatenate(picks, axis=0)
    rank_ref[...] = jnp.concatenate(ranks, axis=0).astype(I32)
    w_ref[...] = jnp.concatenate(weights, axis=0) / wsum * ROUTE_SCALE
    carry[...] = carry[...] + _dot(chosen_b, jnp.ones((tm, LANES), BF16))
    cnt_ref[...] = carry[...]


def _route(h, w_router_t, bias_col):
    t = h.shape[0]
    tm = ROUTE_TM
    col = lambda: pl.BlockSpec((TOP_K, tm), lambda i: (0, i))
    return pl.pallas_call(
        _route_kernel,
        grid=(t // tm,),
        in_specs=[pl.BlockSpec((tm, D_MODEL), lambda i: (i, 0)),
                  pl.BlockSpec((N_EXPERTS, D_MODEL), lambda i: (0, 0)),
                  pl.BlockSpec((N_EXPERTS, 1), lambda i: (0, 0))],
        out_specs=[col(), col(), col(), pl.BlockSpec((N_EXPERTS, LANES), lambda i: (0, 0))],
        out_shape=[jax.ShapeDtypeStruct((TOP_K, t), I32), jax.ShapeDtypeStruct((TOP_K, t), I32),
                   jax.ShapeDtypeStruct((TOP_K, t), F32), jax.ShapeDtypeStruct((N_EXPERTS, LANES), F32)],
        scratch_shapes=[pltpu.VMEM((N_EXPERTS, LANES), F32)],
        compiler_params=_cparams("arbitrary"),
        name="route",
    )(h, w_router_t, bias_col)


DISPATCH_TM = 128


def _row_copy(src, dst, s_row, d_row, sem):
    return pltpu.make_async_copy(src.at[pl.ds(s_row, 1)], dst.at[pl.ds(d_row, 1)], sem)


def _dispatch_kernel(dest_ref, h_ref, xs_ref, sem):
    def issue(t, _):
        for r in range(TOP_K):
            _row_copy(h_ref, xs_ref, t, dest_ref[t * TOP_K + r], sem).start()
        return 0

    lax.fori_loop(0, DISPATCH_TM, issue, 0)

    def drain(t, _):
        for r in range(TOP_K):
            _row_copy(h_ref, xs_ref, 0, 0, sem).wait()
        return 0

    lax.fori_loop(0, DISPATCH_TM, drain, 0)


def _dispatch(h, dest_flat):
    t = h.shape[0]
    return pl.pallas_call(
        _dispatch_kernel,
        grid=(t // DISPATCH_TM,),
        in_specs=[pl.BlockSpec((DISPATCH_TM * TOP_K,), lambda i: (i,), memory_space=pltpu.SMEM),
                  pl.BlockSpec((DISPATCH_TM, D_MODEL), lambda i: (i, 0))],
        out_specs=pl.BlockSpec(memory_space=pl.ANY),
        out_shape=jax.ShapeDtypeStruct((t * TOP_K, D_MODEL), F32),
        scratch_shapes=[pltpu.SemaphoreType.DMA(())],
        compiler_params=_cparams("arbitrary"),
        name="moe_dispatch",
    )(dest_flat, h)


def _expert_kernel(blk_ref, exp_ref, lo_ref, hi_ref, first_ref, x_ref, wg_ref, wu_ref, wd_ref, y_ref):
    del blk_ref, exp_ref
    i = pl.program_id(0)
    xb = x_ref[...].astype(BF16)
    gate = _dot(xb, wg_ref[0].astype(BF16))
    up = _dot(xb, wu_ref[0].astype(BF16))
    act = gate * _sigmoid(gate) * up
    y = _dot(act.astype(BF16), wd_ref[0].astype(BF16))
    row = _iota((ROW_BLOCK, 1), 0)
    y = jnp.where((row >= lo_ref[i]) & (row < hi_ref[i]), y, 0.0)

    @pl.when(first_ref[i] == 1)
    def _():
        y_ref[...] = y

    @pl.when(first_ref[i] == 0)
    def _():
        y_ref[...] = y_ref[...] + y


def _expert_mm(xs, w_gate, w_up, w_down, blk, exp, lo, hi, first):
    n_items = blk.shape[0]
    grid_spec = pltpu.PrefetchScalarGridSpec(
        num_scalar_prefetch=5,
        grid=(n_items,),
        in_specs=[pl.BlockSpec((ROW_BLOCK, D_MODEL), lambda i, blk, exp, lo, hi, fi: (blk[i], 0)),
                  pl.BlockSpec((1, D_MODEL, D_EXPERT), lambda i, blk, exp, lo, hi, fi: (exp[i], 0, 0)),
                  pl.BlockSpec((1, D_MODEL, D_EXPERT), lambda i, blk, exp, lo, hi, fi: (exp[i], 0, 0)),
                  pl.BlockSpec((1, D_EXPERT, D_MODEL), lambda i, blk, exp, lo, hi, fi: (exp[i], 0, 0))],
        out_specs=pl.BlockSpec((ROW_BLOCK, D_MODEL), lambda i, blk, exp, lo, hi, fi: (blk[i], 0)),
    )
    return pl.pallas_call(
        _expert_kernel,
        grid_spec=grid_spec,
        out_shape=jax.ShapeDtypeStruct(xs.shape, F32),
        compiler_params=_cparams("arbitrary"),
        name="moe_experts",
    )(blk, exp, lo, hi, first, xs, w_gate, w_up, w_down)


COMBINE_TM = 128


def _combine_kernel(dest_ref, h_ref, wt_ref, y_ref, wsg_ref, wsu_ref, wsd_ref, g_ref, b_ref, o_ref, buf, sem):
    def issue(t, _):
        for r in range(TOP_K):
            pltpu.make_async_copy(y_ref.at[pl.ds(dest_ref[t * TOP_K + r], 1)], buf.at[r, pl.ds(t, 1)], sem).start()
        return 0

    lax.fori_loop(0, COMBINE_TM, issue, 0)
    h = h_ref[...]
    hb = h.astype(BF16)
    gate = _dot(hb, wsg_ref[...])
    up = _dot(hb, wsu_ref[...])
    f = _dot((gate * _sigmoid(gate) * up).astype(BF16), wsd_ref[...])

    def drain(t, _):
        for r in range(TOP_K):
            pltpu.make_async_copy(y_ref.at[pl.ds(0, 1)], buf.at[r, pl.ds(t, 1)], sem).wait()
        return 0

    lax.fori_loop(0, COMBINE_TM, drain, 0)
    for r in range(TOP_K):
        f = f + wt_ref[:, r:r + 1] * buf[r]
    o_ref[...] = _layer_norm(DN_ALPHA * h + f, g_ref[...], b_ref[...])


def _combine(h, w_tok, dest_flat, y, w_sg, w_su, w_sd, g, b):
    t = h.shape[0]
    tm = COMBINE_TM
    return pl.pallas_call(
        _combine_kernel,
        grid=(t // tm,),
        in_specs=[pl.BlockSpec((tm * TOP_K,), lambda i: (i,), memory_space=pltpu.SMEM),
                  pl.BlockSpec((tm, D_MODEL), lambda i: (i, 0)),
                  pl.BlockSpec((tm, TOP_K), lambda i: (i, 0)),
                  pl.BlockSpec(memory_space=pl.ANY),
                  _const_spec((D_MODEL, D_SHARED)), _const_spec((D_MODEL, D_SHARED)),
                  _const_spec((D_SHARED, D_MODEL)), _const_spec((1, D_MODEL)), _const_spec((1, D_MODEL))],
        out_specs=pl.BlockSpec((tm, D_MODEL), lambda i: (i, 0)),
        out_shape=jax.ShapeDtypeStruct((t, D_MODEL), F32),
        scratch_shapes=[pltpu.VMEM((TOP_K, tm, D_MODEL), F32), pltpu.SemaphoreType.DMA(())],
        compiler_params=_cparams("arbitrary"),
        name="moe_combine",
    )(dest_flat, h, w_tok, y, w_sg, w_su, w_sd, g, b)


def _moe_plan(counts, n_rows):
    n_blocks = n_rows // ROW_BLOCK
    n_items = n_blocks + N_EXPERTS
    start = jnp.cumsum(counts) - counts
    end = start + counts
    first_blk = start // ROW_BLOCK
    last_blk = jnp.maximum(end - 1, start) // ROW_BLOCK
    nb = jnp.where(counts > 0, last_blk - first_blk + 1, 0)
    item_end = jnp.cumsum(nb)
    item_start = item_end - nb
    total = item_end[-1]
    idx = jnp.arange(n_items, dtype=I32)
    last_item = jnp.maximum(total - 1, 0)
    src = jnp.minimum(idx, last_item)
    exp = jnp.minimum(jnp.sum(item_end[None, :] <= src[:, None], axis=1), N_EXPERTS - 1).astype(I32)
    onehot = exp[:, None] == jnp.arange(N_EXPERTS, dtype=I32)[None, :]
    take = lambda table: jnp.sum(jnp.where(onehot, table[None, :], 0), axis=1)
    blk = (take(first_blk) + src - take(item_start)).astype(I32)
    valid = idx < total
    lo = jnp.clip(take(start) - blk * ROW_BLOCK, 0, ROW_BLOCK)
    hi = jnp.clip(take(end) - blk * ROW_BLOCK, 0, ROW_BLOCK)
    first = (valid & (lo == 0)).astype(I32)
    lo = jnp.where(valid, lo, 0).astype(I32)
    hi = jnp.where(valid, hi, 0).astype(I32)
    return start, blk, exp, lo, hi, first


def _moe(h, w_router, router_bias, w_exp_gate, w_exp_up, w_exp_down, w_sg, w_su, w_sd, g, b):
    t = h.shape[0]
    picks, ranks, weights, cnt = _route(h, w_router.T, router_bias.astype(F32).reshape(N_EXPERTS, 1))
    counts = cnt[:, 0].astype(I32)
    start, blk, exp, lo, hi, first = _moe_plan(counts, t * TOP_K)
    experts = jnp.arange(N_EXPERTS, dtype=I32)[:, None, None]
    dest = ranks + jnp.sum(jnp.where(picks[None] == experts, start.astype(I32)[:, None, None], 0), axis=0)
    dest_flat = dest.T.reshape(t * TOP_K)
    xs = _dispatch(h, dest_flat)
    y = _expert_mm(xs, w_exp_gate, w_exp_up, w_exp_down, blk, exp, lo, hi, first)
    return _combine(h, weights.T, dest_flat, y, w_sg, w_su, w_sd, g, b)


def kernel(x_prompt, x_sample, cache_sb_k, cache_sb_v, page_table, state_gdn, state_gdn_conv, cache_mem_k, cache_mem_v, mem_prompt, w_in, sb_logit_bias, conv_w, gdn_A_log, gdn_dt_bias, gdn_norm_w, w_mem_k, w_mem_v, w_br_sb, w_br_gdn, w_br_mem, w_o, ln1_g, ln1_b, w_router, router_bias, w_exp_gate, w_exp_up, w_exp_down, w_sh_gate, w_sh_up, w_sh_down, ln2_g, ln2_b):
    bsz, seq, _ = x_prompt.shape
    db = x_sample.shape[0]
    tp = bsz * seq

    ba_off = 3 * SB_W + GDN_CONV_W + GDN_V_W
    mq_off = ba_off + 2 * GDN_HEADS
    gate_off = mq_off + MEM_W
    w_main = jnp.concatenate([w_in[:, :ba_off], w_in[:, mq_off:gate_off]], axis=1).astype(BF16)
    w_ba = jnp.pad(w_in[:, ba_off:mq_off], ((0, 0), (0, LANES - 2 * GDN_HEADS))).astype(BF16)
    w_gates = w_in[:, gate_off:].astype(BF16)
    w_kv = jnp.concatenate([w_mem_k, w_mem_v], axis=1).astype(BF16)
    w_sb, w_gdn, w_mem, w_ob = (w.astype(BF16) for w in (w_br_sb, w_br_gdn, w_br_mem, w_o))
    w_sg, w_su, w_sd = (w.astype(BF16) for w in (w_sh_gate, w_sh_up, w_sh_down))
    row = lambda v: v.astype(F32).reshape(1, -1)
    head_lane = lambda v: jnp.pad(v.astype(F32), (GDN_HEADS, LANES - 2 * GDN_HEADS)).reshape(1, LANES)
    nw_pair = jnp.tile(gdn_norm_w.astype(F32), 2).reshape(1, LANES)
    bias = sb_logit_bias.astype(F32)

    xp = x_prompt.reshape(tp, D_MODEL)
    q, k, v, gqkv, z, mq, ba = _proj_in(xp, w_main, w_ba, 512)
    mem_k, mem_v = _mem_kv(mem_prompt.reshape(bsz * N_MEM, D_MODEL), w_kv, 512)
    sh = lambda a: a.reshape(bsz, seq, a.shape[-1])
    o_sb = _sb_prompt(sh(q), sh(k), sh(v), bias)
    gq3 = sh(gqkv)
    o_gdn, s_p = _gdn_prompt(gq3, sh(z), sh(ba), conv_w.astype(F32), head_lane(gdn_A_log),
                             head_lane(gdn_dt_bias), nw_pair)
    o_mem = _mem_prompt(sh(mq), mem_k.reshape(bsz, N_MEM, MEM_W), mem_v.reshape(bsz, N_MEM, MEM_W), min(512, seq))
    h_p = _merge_ln(xp, o_sb.reshape(tp, SB_W), o_gdn.reshape(tp, GDN_V_W), o_mem.reshape(tp, MEM_W),
                    w_gates, w_sb, w_gdn, w_mem, w_ob, row(ln1_g), row(ln1_b), 512)

    xs = x_sample.reshape(db, D_MODEL)
    q_s, k_s, v_s, gqkv_s, z_s, mq_s, ba_s = _proj_in(xs, w_main, w_ba, db)
    o_sb_s = _sb_sample(q_s, cache_sb_k, cache_sb_v, page_table, bias)
    o_gdn_s, s_s = _gdn_sample(gqkv_s, state_gdn_conv.astype(F32), ba_s, z_s, state_gdn.astype(F32),
                               conv_w.astype(F32), gdn_A_log, gdn_dt_bias, gdn_norm_w)
    o_mem_s = _mem_sample(mq_s, cache_mem_k, cache_mem_v)
    h_s = _merge_ln(xs, o_sb_s, o_gdn_s, o_mem_s, w_gates, w_sb, w_gdn, w_mem, w_ob, row(ln1_g), row(ln1_b), db)

    y = _moe(jnp.concatenate([h_p, h_s], axis=0), w_router, router_bias, w_exp_gate, w_exp_up, w_exp_down,
             w_sg, w_su, w_sd, row(ln2_g), row(ln2_b))

    heads = lambda a, n, hd: a.reshape(n, -1, hd[0], hd[1])
    return (y[:tp].reshape(bsz, seq, D_MODEL), y[tp:].reshape(db, 1, D_MODEL),
            heads(k, bsz, (SB_HEADS, SB_HEAD_DIM)), heads(v, bsz, (SB_HEADS, SB_HEAD_DIM)),
            s_p, gq3[:, seq - (GDN_CONV - 1):, :],
            mem_k.reshape(bsz, N_MEM, MEM_HEADS, MEM_HEAD_DIM), mem_v.reshape(bsz, N_MEM, MEM_HEADS, MEM_HEAD_DIM),
            heads(k_s, db, (SB_HEADS, SB_HEAD_DIM)), heads(v_s, db, (SB_HEADS, SB_HEAD_DIM)),
            s_s, jnp.concatenate([state_gdn_conv[:, 1:, :], gqkv_s[:, None, :]], axis=1))
```

```python
import functools

import jax
import jax.numpy as jnp
from jax import lax
from jax.experimental import pallas as pl
from jax.experimental.pallas import tpu as pltpu

F32 = jnp.float32
BF16 = jnp.bfloat16
I32 = jnp.int32

D_MODEL = 1024
PAGE_SIZE = 128
SB_HEADS = 8
SB_HEAD_DIM = 64
SB_W = SB_HEADS * SB_HEAD_DIM
GDN_HEADS = 8
GDN_K_DIM = 64
GDN_V_DIM = 64
GDN_CONV = 4
GDN_QK_W = GDN_HEADS * GDN_K_DIM
GDN_V_W = GDN_HEADS * GDN_V_DIM
GDN_CONV_W = 2 * GDN_QK_W + GDN_V_W
N_MEM = 256
MEM_HEADS = 4
MEM_HEAD_DIM = 128
MEM_W = MEM_HEADS * MEM_HEAD_DIM
N_EXPERTS = 64
TOP_K = 8
N_GROUPS = 8
TOPK_GROUPS = 4
D_EXPERT = 256
D_SHARED = 256
ROUTE_SCALE = 2.5
DEPTH = 1
DN_ALPHA = (2 * DEPTH) ** 0.25
LN_EPS = 1e-5
RMS_EPS = 1e-6
L2_EPS = 1e-6

LANES = 128
CHUNK = 128
ROW_BLOCK = 256
VMEM_LIMIT = 56 * 1024 * 1024


def _cparams(*sem):
    return pltpu.CompilerParams(dimension_semantics=sem, vmem_limit_bytes=VMEM_LIMIT)


def _dot(a, b):
    return jnp.dot(a, b, preferred_element_type=F32)


def _dot_nt(a, b):
    return lax.dot_general(a, b, (((1,), (1,)), ((), ())), preferred_element_type=F32)


def _split2(x):
    hi = x.astype(BF16)
    lo = (x - hi.astype(F32)).astype(BF16)
    return hi, lo


def _split3(x):
    hi = x.astype(BF16)
    r = x - hi.astype(F32)
    mid = r.astype(BF16)
    lo = (r - mid.astype(F32)).astype(BF16)
    return hi, mid, lo


def _dot_xc(x, m):
    return _dot(jnp.concatenate(_split3(x), axis=1), jnp.concatenate([m, m, m], axis=0))


def _dot_cx(m, x):
    return _dot(jnp.concatenate([m, m, m], axis=1), jnp.concatenate(_split3(x), axis=0))


def _dot_ff(a, b):
    ah, al = _split2(a)
    bh, bl = _split2(b)
    return _dot(jnp.concatenate([ah, ah, al], axis=1), jnp.concatenate([bh, bl, bh], axis=0))


def _sigmoid(x):
    return 1.0 / (1.0 + jnp.exp(-x))


def _softplus(x):
    return jnp.maximum(x, 0.0) + jnp.log(1.0 + jnp.exp(-jnp.abs(x)))


def _iota(shape, dim):
    return lax.broadcasted_iota(I32, shape, dim)


def _const_spec(shape):
    nd = len(shape)
    return pl.BlockSpec(shape, lambda *_: (0,) * nd, pipeline_mode=pl.Buffered(1))


def _layer_norm(x, g, b):
    mu = jnp.mean(x, axis=-1, keepdims=True)
    xc = x - mu
    var = jnp.mean(xc * xc, axis=-1, keepdims=True)
    return xc * lax.rsqrt(var + LN_EPS) * g + b


def _proj_kernel(x_ref, w_ref, wba_ref, q_ref, k_ref, v_ref, g_ref, z_ref, mq_ref, ba_ref):
    xb = x_ref[...].astype(BF16)
    q_ref[...] = (_dot(xb, w_ref[:, 0:512]) * 0.125).astype(BF16)
    k_ref[...] = _dot(xb, w_ref[:, 512:1024])
    v_ref[...] = _dot(xb, w_ref[:, 1024:1536])
    for c in range(3):
        g_ref[:, c * 512:(c + 1) * 512] = _dot(xb, w_ref[:, 1536 + c * 512:2048 + c * 512])
    z_ref[...] = _dot(xb, w_ref[:, 3072:3584]).astype(BF16)
    mq_ref[...] = _dot(xb, w_ref[:, 3584:4096]).astype(BF16)
    ba_ref[...] = _dot(xb, wba_ref[...])


def _proj_in(x2d, w_main, w_ba, tm):
    t = x2d.shape[0]
    row = lambda w: pl.BlockSpec((tm, w), lambda i: (i, 0))
    return pl.pallas_call(
        _proj_kernel,
        grid=(t // tm,),
        in_specs=[row(D_MODEL), _const_spec((D_MODEL, 4096)), _const_spec((D_MODEL, LANES))],
        out_specs=[row(512), row(512), row(512), row(1536), row(512), row(512), row(LANES)],
        out_shape=[jax.ShapeDtypeStruct((t, 512), BF16), jax.ShapeDtypeStruct((t, 512), F32),
                   jax.ShapeDtypeStruct((t, 512), F32), jax.ShapeDtypeStruct((t, 1536), F32),
                   jax.ShapeDtypeStruct((t, 512), BF16), jax.ShapeDtypeStruct((t, 512), BF16),
                   jax.ShapeDtypeStruct((t, LANES), F32)],
        compiler_params=_cparams("parallel"),
        name="proj_in",
    )(x2d, w_main, w_ba)


def _memkv_kernel(x_ref, w_ref, k_ref, v_ref):
    xb = x_ref[...].astype(BF16)
    k_ref[...] = _dot(xb, w_ref[:, 0:512])
    v_ref[...] = _dot(xb, w_ref[:, 512:1024])


def _mem_kv(x2d, w_kv, tm):
    t = x2d.shape[0]
    row = lambda w: pl.BlockSpec((tm, w), lambda i: (i, 0))
    return pl.pallas_call(
        _memkv_kernel,
        grid=(t // tm,),
        in_specs=[row(D_MODEL), _const_spec((D_MODEL, 1024))],
        out_specs=[row(512), row(512)],
        out_shape=[jax.ShapeDtypeStruct((t, 512), F32)] * 2,
        compiler_params=_cparams("parallel"),
        name="mem_kv",
    )(x2d, w_kv)


def _cumsum_matrix():
    r = _iota((2 * CHUNK, 2 * CHUNK), 0) % CHUNK
    c = _iota((2 * CHUNK, 2 * CHUNK), 1)
    return jnp.where((c >= CHUNK) | (r > c), 1.0, 0.0).astype(BF16)


SB_QT = 512


SB_KB = 2


def _sb_prompt_kernel(qt, bias_ref, q_ref, k_ref, v_ref, o_ref, qs, kms, vms, carry, acc):
    hp = pl.program_id(1)
    seq = q_ref.shape[1]
    nblk = seq // CHUNK
    span = SB_KB * CHUNK
    first = _iota((CHUNK, LANES), 1) < SB_HEAD_DIM
    tt = _cumsum_matrix()
    dist = _iota((qt, LANES), 0) - _iota((qt, LANES), 1)
    biases = (bias_ref[2 * hp], bias_ref[2 * hp + 1])

    def stage(i, _):
        rows = pl.ds(pl.multiple_of(i * CHUNK, CHUNK), CHUNK)
        qs[rows, :] = q_ref[0, rows, :]
        base = pl.multiple_of(i * 2 * CHUNK, 2 * CHUNK)
        for src, dst in ((k_ref, kms), (v_ref, vms)):
            x = src[0, rows, :]
            dst[pl.ds(base, CHUNK), :] = jnp.where(first, x, 0.0).astype(BF16)
            dst[pl.ds(base + CHUNK, CHUNK), :] = jnp.where(first, 0.0, x).astype(BF16)
        zero = jnp.zeros((CHUNK, LANES), F32)
        carry[0, rows, :] = zero
        carry[1, rows, :] = zero
        acc[rows, :] = zero
        return 0

    lax.fori_loop(0, nblk, stage, 0)

    def kstep(pp, _):
        p = nblk // SB_KB - 1 - pp
        key0 = p * span
        staged = pl.ds(pl.multiple_of(p * 2 * span, 2 * span), 2 * span)
        km = kms[staged, :]
        vm = vms[staged, :]

        def qtile(t, _):
            row0 = pl.multiple_of(t * qt, qt)
            rows = pl.ds(row0, qt)
            z_all = _dot_nt(qs[rows, :], km)
            zs, sps, parts, masks = [], [], [], []
            for g in range(2 * SB_KB):
                causal = dist > key0 + (g // 2) * CHUNK - row0
                z = z_all[:, g * CHUNK:(g + 1) * CHUNK] + biases[g % 2]
                sp = _softplus(z)
                hi, lo = _split2(jnp.where(causal, sp, 0.0))
                zs.append(z)
                sps.append(sp)
                masks.append(causal)
                parts.append(jnp.concatenate([hi, lo], axis=1))
            r_all = _dot(jnp.concatenate(parts, axis=0), tt)
            a = [None] * (2 * SB_KB)
            for h in range(2):
                c = carry[h, rows, :]
                for blk in reversed(range(SB_KB)):
                    g = 2 * blk + h
                    r = r_all[g * qt:(g + 1) * qt]
                    e = jnp.exp(zs[g] - sps[g] - r[:, :CHUNK] - c)
                    a[g] = jnp.where(masks[g], e, 0.0).astype(BF16)
                    c = c + r[:, CHUNK:]
                carry[h, rows, :] = c
            acc[rows, :] = acc[rows, :] + _dot(jnp.concatenate(a, axis=1), vm)
            return 0

        lax.fori_loop(key0 // qt, seq // qt, qtile, 0)
        return 0

    lax.fori_loop(0, nblk // SB_KB, kstep, 0)

    def write_out(i, _):
        rows = pl.ds(pl.multiple_of(i * CHUNK, CHUNK), CHUNK)
        o_ref[0, rows, :] = acc[rows, :].astype(BF16)
        return 0

    lax.fori_loop(0, nblk, write_out, 0)


def _sb_prompt(q, k, v, bias):
    b, seq, _ = q.shape
    qt = min(SB_QT, seq)
    blk = lambda: pl.BlockSpec((1, seq, LANES), lambda i, j: (i, 0, j))
    return pl.pallas_call(
        functools.partial(_sb_prompt_kernel, qt),
        grid=(b, SB_W // LANES),
        in_specs=[pl.BlockSpec(memory_space=pltpu.SMEM), blk(), blk(), blk()],
        out_specs=blk(),
        out_shape=jax.ShapeDtypeStruct((b, seq, SB_W), BF16),
        scratch_shapes=[pltpu.VMEM((seq, LANES), BF16), pltpu.VMEM((2 * seq, LANES), BF16),
                        pltpu.VMEM((2 * seq, LANES), BF16), pltpu.VMEM((2, seq, LANES), F32),
                        pltpu.VMEM((seq, LANES), F32)],
        compiler_params=_cparams("parallel", "parallel"),
        name="sb_prompt",
    )(bias, q, k, v)


def _sb_sample_kernel(n_pages, pt_ref, q_ref, bias_ref, *refs):
    del pt_ref
    k_refs = refs[:n_pages]
    v_refs = refs[n_pages:2 * n_pages]
    o_ref = refs[2 * n_pages]
    head_of_lane = _iota((SB_HEADS, SB_W), 1) // SB_HEAD_DIM
    own = head_of_lane == _iota((SB_HEADS, SB_W), 0)
    qm = jnp.where(own, q_ref[0].astype(F32), 0.0).astype(BF16)
    pages = range(n_pages)
    kcat = jnp.concatenate([k_refs[p][0].astype(BF16) for p in pages], axis=1)
    z = _dot(qm, kcat)
    zr = jnp.concatenate([z[:, p * PAGE_SIZE:(p + 1) * PAGE_SIZE] + bias_ref[...] for p in pages], axis=0)
    sp = _softplus(zr)
    hi, lo = _split2(sp)
    r = _dot(jnp.concatenate([hi, lo], axis=1), _cumsum_matrix())
    carries = [None] * n_pages
    c = jnp.zeros((SB_HEADS, LANES), F32)
    for p in reversed(pages):
        carries[p] = c
        c = c + r[p * SB_HEADS:(p + 1) * SB_HEADS, PAGE_SIZE:]
    a = jnp.exp(zr - sp - r[:, :PAGE_SIZE] - jnp.concatenate(carries, axis=0))
    acat = jnp.concatenate([a[p * SB_HEADS:(p + 1) * SB_HEADS] for p in pages], axis=1).astype(BF16)
    vcat = jnp.concatenate([v_refs[p][0].astype(BF16) for p in pages], axis=1)
    o = _dot_nt(acat, vcat)
    o_ref[0] = jnp.sum(jnp.where(own, o, 0.0), axis=0, keepdims=True).astype(BF16)


def _sb_sample(q, cache_k, cache_v, page_table, bias):
    db, n_pages = page_table.shape
    n_pool = cache_k.shape[0]
    ck = cache_k.transpose(0, 2, 3, 1).reshape(n_pool, SB_W, PAGE_SIZE)
    cv = cache_v.transpose(0, 2, 3, 1).reshape(n_pool, SB_W, PAGE_SIZE)
    bias2d = jnp.broadcast_to(bias.astype(F32)[:, None], (SB_HEADS, LANES))

    def page_spec(p):
        return pl.BlockSpec((1, SB_W, PAGE_SIZE), lambda i, pt: (pt[i, p], 0, 0))

    grid_spec = pltpu.PrefetchScalarGridSpec(
        num_scalar_prefetch=1,
        grid=(db,),
        in_specs=[pl.BlockSpec((1, 1, SB_W), lambda i, pt: (i, 0, 0)),
                  pl.BlockSpec((SB_HEADS, LANES), lambda i, pt: (0, 0))]
        + [page_spec(p) for p in range(n_pages)] * 2,
        out_specs=pl.BlockSpec((1, 1, SB_W), lambda i, pt: (i, 0, 0)),
    )
    out = pl.pallas_call(
        functools.partial(_sb_sample_kernel, n_pages),
        grid_spec=grid_spec,
        out_shape=jax.ShapeDtypeStruct((db, 1, SB_W), BF16),
        compiler_params=_cparams("parallel"),
        name="sb_sample",
    )(page_table, q.reshape(db, 1, SB_W), bias2d, *([ck] * n_pages), *([cv] * n_pages))
    return out.reshape(db, SB_W)


def _head_ones():
    r = _iota((LANES, LANES), 0) // GDN_K_DIM
    c = _iota((LANES, LANES), 1) // GDN_K_DIM
    return jnp.where(r == c, 1.0, 0.0).astype(BF16)


def _unit_lower_solve(a, rhs):
    p = -a
    y = rhs
    for _ in range(6):
        r = _dot_ff(p, jnp.concatenate([p, y], axis=1))
        y = y + r[:, CHUNK:]
        p = r[:, :CHUNK]
    return y + _dot_ff(p, y)


GDN_TILE = 512


def _gdn_prompt_parts(tile, hp, qp_ref, kp_ref, vp_ref, z_ref, ba_ref, cwq_ref, cwk_ref, cwv_ref, alog_ref, dtb_ref,
                      nw_ref, o_ref, s_ref, qs, ks, vs, exs, os_):
    seq = qp_ref.shape[1]
    pad = 8
    row = _iota((CHUNK, CHUNK), 0)
    col = _iota((CHUNK, CHUNK), 1)
    lower = row >= col
    strict = row > col
    first = col < GDN_K_DIM
    same_head = (row // GDN_K_DIM) == (col // GDN_K_DIM)
    head_ones = _head_ones()
    lower_ones = jnp.where(lower, 1.0, 0.0).astype(BF16)
    er = _iota((LANES, 4 * LANES), 0)
    ec = _iota((LANES, 4 * LANES), 1)
    sec = ec // LANES
    pair_head = 2 * hp + (ec % LANES) // GDN_K_DIM
    want = jnp.where(sec == 0, pair_head,
                     jnp.where(sec == 1, GDN_HEADS + pair_head,
                               jnp.where(sec == 2, GDN_HEADS + 2 * hp, GDN_HEADS + 2 * hp + 1)))
    expand = jnp.where(er == want, 1.0, 0.0).astype(BF16)
    lane_row = _iota((1, LANES), 1)
    neg_a = -jnp.exp(alog_ref[...])
    scale = GDN_K_DIM ** -0.5

    n_sub = tile // CHUNK

    def conv(scr, cw_ref, r0):
        w = scr[pl.ds(r0, tile + pad), :]
        y = (w[5:5 + tile] * cw_ref[0:1, :] + w[6:6 + tile] * cw_ref[1:2, :]
             + w[7:7 + tile] * cw_ref[2:3, :] + w[8:8 + tile] * cw_ref[3:4, :])
        return y * _sigmoid(y)

    def prepare(tt, _):
        r0 = pl.multiple_of((seq // tile - 1 - tt) * tile, tile)
        rows = pl.ds(r0, tile)
        q = conv(qs, cwq_ref, r0)
        k = conv(ks, cwk_ref, r0)
        v = conv(vs, cwv_ref, r0)
        ss = _dot_xc(jnp.concatenate([q * q, k * k], axis=0), head_ones)
        qs[pl.ds(r0 + pad, tile), :] = q * lax.rsqrt(ss[:tile] + L2_EPS) * scale
        ks[pl.ds(r0 + pad, tile), :] = k * lax.rsqrt(ss[tile:] + L2_EPS)
        vs[pl.ds(r0 + pad, tile), :] = v
        raw = ba_ref[0, rows, :]
        beta_raw = _sigmoid(raw)
        g_raw = neg_a * _softplus(raw + dtb_ref[...])
        g_wide = jnp.concatenate([g_raw[i * CHUNK:(i + 1) * CHUNK] for i in range(n_sub)], axis=1)
        c_wide = _dot_cx(lower_ones, g_wide)
        gcum = jnp.concatenate([c_wide[:, i * LANES:(i + 1) * LANES] for i in range(n_sub)], axis=0)
        exs[rows, :] = _dot_xc(jnp.where(lane_row < GDN_HEADS, beta_raw, gcum), expand)
        return 0

    def setup():
        for scr, src in ((qs, qp_ref), (ks, kp_ref), (vs, vp_ref)):
            scr[0:pad, :] = jnp.zeros((pad, LANES), F32)
            scr[pad:seq + pad, :] = src[0]
        lax.fori_loop(0, seq // tile, prepare, 0)

    def within_chunk(c):
        r0 = pl.multiple_of(c * CHUNK, CHUNK)
        rows = pl.ds(r0, CHUNK)
        srows = pl.ds(r0 + pad, CHUNK)
        qn = qs[srows, :]
        kn = ks[srows, :]
        v = vs[srows, :]
        ex = exs[rows, :]
        beta_p = ex[:, 0:LANES]
        gc_p = ex[:, LANES:2 * LANES]
        eg_p = jnp.exp(gc_p)
        g_last = gc_p[CHUNK - 1:CHUNK, :]
        kb = kn * beta_p
        lhs = jnp.concatenate([jnp.where(first, kb, 0.0), jnp.where(first, qn, 0.0),
                               jnp.where(first, 0.0, kb), jnp.where(first, 0.0, qn)], axis=0).astype(BF16)
        kq = _dot_nt(lhs, kn.astype(BF16))
        rhs = jnp.concatenate([v * beta_p, kb * eg_p], axis=1)
        sols = []
        qks = []
        for h in range(2):
            g1 = ex[:, (2 + h) * LANES:(3 + h) * LANES]
            diff = g1 - g1.T
            dec = jnp.where(lower, jnp.exp(jnp.where(lower, diff, 0.0)), 0.0)
            kk = kq[2 * h * CHUNK:(2 * h + 1) * CHUNK]
            sols.append(_unit_lower_solve(jnp.where(strict, kk * dec, 0.0), rhs))
            qks.append((kq[(2 * h + 1) * CHUNK:(2 * h + 2) * CHUNK] * dec).astype(BF16))
        first2 = jnp.concatenate([first, first], axis=1)
        sol = jnp.where(first2, sols[0], sols[1])
        u = sol[:, 0:LANES]
        wq = jnp.concatenate([sol[:, LANES:2 * LANES], qn * eg_p], axis=0).astype(BF16)
        k_tail_t = (kn * jnp.exp(g_last - gc_p)).T.astype(BF16)
        return rows, u, wq, jnp.concatenate(qks, axis=1), k_tail_t, jnp.exp(g_last)

    def advance(s, parts):
        rows, u, wq, qk2, k_tail_t, decay = parts
        ws = _dot(wq, s.astype(BF16))
        v_new = u - ws[:CHUNK]
        v_heads = jnp.concatenate([jnp.where(first, v_new, 0.0), jnp.where(first, 0.0, v_new)], axis=0)
        os_[rows, :] = ws[CHUNK:] + _dot(qk2, v_heads.astype(BF16))
        upd = _dot(k_tail_t, v_new.astype(BF16))
        return s * decay + jnp.where(same_head, upd, 0.0)

    def normalize(tt, _):
        rows = pl.ds(pl.multiple_of(tt * tile, tile), tile)
        o = os_[rows, :]
        ms = _dot_xc(o * o, head_ones) * (1.0 / GDN_V_DIM)
        zf = z_ref[0, rows, :].astype(F32)
        o_ref[0, rows, :] = (o * lax.rsqrt(ms + RMS_EPS) * nw_ref[...] * (zf * _sigmoid(zf))).astype(BF16)
        return 0

    def finish(s):
        s_ref[0, 0] = s[0:GDN_K_DIM, 0:GDN_V_DIM]
        s_ref[0, 1] = s[GDN_K_DIM:, GDN_V_DIM:]
        lax.fori_loop(0, seq // tile, normalize, 0)

    return setup, within_chunk, advance, finish


GDN_GROUP = 4


def _gdn_prompt_kernel(tile, group, *refs):
    setup, within_chunk, advance, finish = _gdn_prompt_parts(tile, pl.program_id(1), *refs)
    seq = refs[0].shape[1]
    setup()

    def chunk_group(cg, s):
        parts = [within_chunk(cg * group + i) for i in range(group)]
        for p in parts:
            s = advance(s, p)
        return s

    finish(lax.fori_loop(0, seq // (CHUNK * group), chunk_group, jnp.zeros((LANES, LANES), F32)))


def _gdn_prompt(gqkv, z, ba, conv_w, alog_lane, dtb_lane, nw_lane):
    b, seq, _ = gqkv.shape
    n_pair = GDN_HEADS // 2
    blk = lambda off: pl.BlockSpec((1, seq, LANES), lambda i, j: (i, 0, off + j))
    cw = lambda off: pl.BlockSpec((GDN_CONV, LANES), lambda i, j: (0, off + j))
    lane_vec = lambda: pl.BlockSpec((1, LANES), lambda i, j: (0, 0))
    return pl.pallas_call(
        functools.partial(_gdn_prompt_kernel, min(GDN_TILE, seq), min(GDN_GROUP, seq // CHUNK)),
        grid=(b, n_pair),
        in_specs=[blk(0), blk(n_pair), blk(2 * n_pair), blk(0),
                  pl.BlockSpec((1, seq, LANES), lambda i, j: (i, 0, 0)),
                  cw(0), cw(n_pair), cw(2 * n_pair), lane_vec(), lane_vec(), lane_vec()],
        out_specs=[blk(0), pl.BlockSpec((1, 2, GDN_K_DIM, GDN_V_DIM), lambda i, j: (i, j, 0, 0))],
        out_shape=[jax.ShapeDtypeStruct((b, seq, GDN_V_W), BF16),
                   jax.ShapeDtypeStruct((b, GDN_HEADS, GDN_K_DIM, GDN_V_DIM), F32)],
        scratch_shapes=[pltpu.VMEM((seq + 8, LANES), F32)] * 3
        + [pltpu.VMEM((seq, 4 * LANES), F32), pltpu.VMEM((seq, LANES), F32)],
        compiler_params=_cparams("parallel", "parallel"),
        name="gdn_prompt",
    )(gqkv, gqkv, gqkv, z, ba, conv_w, conv_w, conv_w, alog_lane, dtb_lane, nw_lane)


GDN_BT = 8


def _gdn_sample_kernel(g_ref, c0_ref, gt_ref, c0t_ref, ba_ref, bat_ref, z_ref, s_ref, cw_ref, cwt_ref,
                       alogc_ref, dtbc_ref, nw_ref, o_ref, sn_ref, o_scr):
    scale = GDN_K_DIM ** -0.5
    y = (c0_ref[:, 0, :] * cw_ref[0:1, :] + c0_ref[:, 1, :] * cw_ref[1:2, :]
         + c0_ref[:, 2, :] * cw_ref[2:3, :] + g_ref[...] * cw_ref[3:4, :])
    v_row = (y * _sigmoid(y))[:, 2 * GDN_QK_W:]
    raw = ba_ref[...]
    er = _iota((LANES, GDN_V_W), 0)
    ec = _iota((LANES, GDN_V_W), 1)
    expand = jnp.where(er == ec // GDN_V_DIM, 1.0, 0.0).astype(BF16)
    beta_row = _dot_xc(_sigmoid(raw), expand)
    u_row = v_row * beta_row
    yt = (c0t_ref[0, 0] * cwt_ref[0] + c0t_ref[0, 1] * cwt_ref[1] + c0t_ref[0, 2] * cwt_ref[2]
          + gt_ref[0] * cwt_ref[3])
    yt = yt * _sigmoid(yt)
    rawt = bat_ref[0]
    beta_c = _sigmoid(rawt[0:GDN_HEADS])
    g_c = -jnp.exp(alogc_ref[...]) * _softplus(rawt[GDN_HEADS:] + dtbc_ref[...])
    eg_c = jnp.exp(g_c)
    for h in range(GDN_HEADS):
        qh = yt[h * GDN_K_DIM:(h + 1) * GDN_K_DIM]
        kh = yt[GDN_QK_W + h * GDN_K_DIM:GDN_QK_W + (h + 1) * GDN_K_DIM]
        qn = qh * lax.rsqrt(jnp.sum(qh * qh, axis=0, keepdims=True) + L2_EPS) * scale
        kn = kh * lax.rsqrt(jnp.sum(kh * kh, axis=0, keepdims=True) + L2_EPS)
        eg = eg_c[h:h + 1, :]
        wn = kn * (beta_c[h:h + 1, :] * eg)
        qd = qn * eg
        qk = jnp.sum(qn * kn, axis=0, keepdims=True)
        for b in range(GDN_BT):
            s = s_ref[b, h]
            u = u_row[b:b + 1, h * GDN_V_DIM:(h + 1) * GDN_V_DIM]
            v_new = u - jnp.sum(wn[:, b:b + 1] * s, axis=0, keepdims=True)
            o = jnp.sum(qd[:, b:b + 1] * s, axis=0, keepdims=True) + qk[:, b:b + 1] * v_new
            sn_ref[b, h] = s * eg[:, b:b + 1] + kn[:, b:b + 1] * v_new
            o_scr[b:b + 1, h * GDN_V_DIM:(h + 1) * GDN_V_DIM] = o
    o = o_scr[...]
    r = _iota((GDN_V_W, GDN_V_W), 0) // GDN_V_DIM
    c = _iota((GDN_V_W, GDN_V_W), 1) // GDN_V_DIM
    ms = _dot_xc(o * o, jnp.where(r == c, 1.0, 0.0).astype(BF16)) * (1.0 / GDN_V_DIM)
    zf = z_ref[...].astype(F32)
    o_ref[...] = (o * lax.rsqrt(ms + RMS_EPS) * nw_ref[...] * (zf * _sigmoid(zf))).astype(BF16)


def _gdn_sample(gqkv, conv0, ba, z, state, conv_w, alog, dtb, norm_w):
    db = gqkv.shape[0]
    nt = db // GDN_BT
    qk_w = 2 * GDN_QK_W
    gt = gqkv[:, :qk_w].T.reshape(qk_w, nt, GDN_BT).transpose(1, 0, 2)
    c0t = conv0[:, :, :qk_w].transpose(1, 2, 0).reshape(3, qk_w, nt, GDN_BT).transpose(2, 0, 1, 3)
    bat = ba[:, :2 * GDN_HEADS].T.reshape(2 * GDN_HEADS, nt, GDN_BT).transpose(1, 0, 2)
    cwt = conv_w[:, :qk_w].reshape(GDN_CONV, qk_w, 1)
    alog_c = alog.astype(F32).reshape(GDN_HEADS, 1)
    dtb_c = dtb.astype(F32).reshape(GDN_HEADS, 1)
    nw = jnp.tile(norm_w.astype(F32), GDN_HEADS).reshape(1, GDN_V_W)
    full = lambda shape: pl.BlockSpec(shape, lambda i: (0,) * len(shape))
    return pl.pallas_call(
        _gdn_sample_kernel,
        grid=(nt,),
        in_specs=[pl.BlockSpec((GDN_BT, GDN_CONV_W), lambda i: (i, 0)),
                  pl.BlockSpec((GDN_BT, GDN_CONV - 1, GDN_CONV_W), lambda i: (i, 0, 0)),
                  pl.BlockSpec((1, qk_w, GDN_BT), lambda i: (i, 0, 0)),
                  pl.BlockSpec((1, GDN_CONV - 1, qk_w, GDN_BT), lambda i: (i, 0, 0, 0)),
                  pl.BlockSpec((GDN_BT, LANES), lambda i: (i, 0)),
                  pl.BlockSpec((1, 2 * GDN_HEADS, GDN_BT), lambda i: (i, 0, 0)),
                  pl.BlockSpec((GDN_BT, GDN_V_W), lambda i: (i, 0)),
                  pl.BlockSpec((GDN_BT, GDN_HEADS, GDN_K_DIM, GDN_V_DIM), lambda i: (i, 0, 0, 0)),
                  full((GDN_CONV, GDN_CONV_W)), full((GDN_CONV, qk_w, 1)),
                  full((GDN_HEADS, 1)), full((GDN_HEADS, 1)), full((1, GDN_V_W))],
        out_specs=[pl.BlockSpec((GDN_BT, GDN_V_W), lambda i: (i, 0)),
                   pl.BlockSpec((GDN_BT, GDN_HEADS, GDN_K_DIM, GDN_V_DIM), lambda i: (i, 0, 0, 0))],
        out_shape=[jax.ShapeDtypeStruct((db, GDN_V_W), BF16),
                   jax.ShapeDtypeStruct(state.shape, F32)],
        scratch_shapes=[pltpu.VMEM((GDN_BT, GDN_V_W), F32)],
        compiler_params=_cparams("parallel"),
        name="gdn_sample",
    )(gqkv, conv0, gt, c0t, ba, bat, z, state, conv_w, cwt, alog_c, dtb_c, nw)


def _mem_prompt_kernel(q_ref, k_ref, v_ref, o_ref):
    scale = MEM_HEAD_DIM ** -0.5
    for h in range(MEM_HEADS):
        lanes = slice(h * MEM_HEAD_DIM, (h + 1) * MEM_HEAD_DIM)
        s = _dot_nt(q_ref[0, :, lanes], k_ref[0, :, lanes].astype(BF16)) * scale
        m = jnp.max(s, axis=-1, keepdims=True)
        e = jnp.exp(s - m)
        p = e / jnp.sum(e, axis=-1, keepdims=True)
        o_ref[0, :, lanes] = _dot(p.astype(BF16), v_ref[0, :, lanes].astype(BF16)).astype(BF16)


def _mem_prompt(q, mem_k, mem_v, tq):
    b, seq, _ = q.shape
    return pl.pallas_call(
        _mem_prompt_kernel,
        grid=(b, seq // tq),
        in_specs=[pl.BlockSpec((1, tq, MEM_W), lambda i, j: (i, j, 0)),
                  pl.BlockSpec((1, N_MEM, MEM_W), lambda i, j: (i, 0, 0)),
                  pl.BlockSpec((1, N_MEM, MEM_W), lambda i, j: (i, 0, 0))],
        out_specs=pl.BlockSpec((1, tq, MEM_W), lambda i, j: (i, j, 0)),
        out_shape=jax.ShapeDtypeStruct((b, seq, MEM_W), BF16),
        compiler_params=_cparams("parallel", "parallel"),
        name="mem_prompt",
    )(q, mem_k, mem_v)


MEM_BT = 4


def _mem_sample_kernel(q_ref, k_ref, v_ref, o_ref):
    scale = MEM_HEAD_DIM ** -0.5
    rows = 8
    head_of_lane = _iota((rows, MEM_W), 1) // MEM_HEAD_DIM
    own = head_of_lane == _iota((rows, MEM_W), 0)
    for b in range(MEM_BT):
        qm = jnp.where(own, q_ref[b].astype(F32), 0.0).astype(BF16)
        s = _dot_nt(qm, k_ref[b].astype(BF16)) * scale
        m = jnp.max(s, axis=-1, keepdims=True)
        e = jnp.exp(s - m)
        p = e / jnp.sum(e, axis=-1, keepdims=True)
        o = _dot(p.astype(BF16), v_ref[b].astype(BF16))
        o_ref[b] = jnp.sum(jnp.where(own, o, 0.0), axis=0, keepdims=True).astype(BF16)


def _mem_sample(q, mem_k, mem_v):
    db = q.shape[0]
    out = pl.pallas_call(
        _mem_sample_kernel,
        grid=(db // MEM_BT,),
        in_specs=[pl.BlockSpec((MEM_BT, 1, MEM_W), lambda i: (i, 0, 0)),
                  pl.BlockSpec((MEM_BT, N_MEM, MEM_W), lambda i: (i, 0, 0)),
                  pl.BlockSpec((MEM_BT, N_MEM, MEM_W), lambda i: (i, 0, 0))],
        out_specs=pl.BlockSpec((MEM_BT, 1, MEM_W), lambda i: (i, 0, 0)),
        out_shape=jax.ShapeDtypeStruct((db, 1, MEM_W), BF16),
        compiler_params=_cparams("parallel"),
        name="mem_sample",
    )(q.reshape(db, 1, MEM_W), mem_k.reshape(db, N_MEM, MEM_W), mem_v.reshape(db, N_MEM, MEM_W))
    return out.reshape(db, MEM_W)


def _merge_kernel(x_ref, osb_ref, ogdn_ref, omem_ref, wg_ref, wsb_ref, wgdn_ref, wmem_ref, wo_ref, g_ref, b_ref,
                  h_ref):
    x = x_ref[...]
    xb = x.astype(BF16)
    merged = None
    for k, (o_ref, w_ref) in enumerate(((osb_ref, wsb_ref), (ogdn_ref, wgdn_ref), (omem_ref, wmem_ref))):
        gate = _sigmoid(_dot(xb, wg_ref[:, k * D_MODEL:(k + 1) * D_MODEL]))
        term = gate * _dot(o_ref[...], w_ref[...])
        merged = term if merged is None else merged + term
    pre = DN_ALPHA * x + _dot(merged.astype(BF16), wo_ref[...])
    h_ref[...] = _layer_norm(pre, g_ref[...], b_ref[...])


def _merge_ln(x2d, o_sb, o_gdn, o_mem, w_gates, w_sb, w_gdn, w_mem, w_o, g, b, tm):
    t = x2d.shape[0]
    row = lambda w: pl.BlockSpec((tm, w), lambda i: (i, 0))
    return pl.pallas_call(
        _merge_kernel,
        grid=(t // tm,),
        in_specs=[row(D_MODEL), row(512), row(512), row(512),
                  _const_spec((D_MODEL, 3 * D_MODEL)), _const_spec((512, D_MODEL)), _const_spec((512, D_MODEL)),
                  _const_spec((512, D_MODEL)), _const_spec((D_MODEL, D_MODEL)),
                  _const_spec((1, D_MODEL)), _const_spec((1, D_MODEL))],
        out_specs=row(D_MODEL),
        out_shape=jax.ShapeDtypeStruct((t, D_MODEL), F32),
        compiler_params=_cparams("parallel"),
        name="merge_ln",
    )(x2d, o_sb, o_gdn, o_mem, w_gates, w_sb, w_gdn, w_mem, w_o, g, b)


ROUTE_TM = 128
GROUP_SIZE = N_EXPERTS // N_GROUPS


def _first_max(x, idx, n):
    m = jnp.max(x, axis=0, keepdims=True)
    first = jnp.min(jnp.where(x == m, idx, n), axis=0, keepdims=True)
    return m, first


def _route_kernel(h_ref, wr_ref, bias_ref, e_ref, rank_ref, w_ref, cnt_ref, carry):
    i = pl.program_id(0)

    @pl.when(i == 0)
    def _():
        carry[...] = jnp.zeros_like(carry)

    tm = h_ref.shape[0]
    neg = -jnp.inf
    hh, hl = _split2(h_ref[...])
    wh, wl = _split2(wr_ref[...])
    logits = _dot_nt(wh, hh) + _dot_nt(wh, hl) + _dot_nt(wl, hh)
    scores = _sigmoid(logits)
    biased = scores + bias_ref[...]
    grp = []
    sub = _iota((GROUP_SIZE, tm), 0)
    for g in range(N_GROUPS):
        xg = biased[g * GROUP_SIZE:(g + 1) * GROUP_SIZE]
        m1, f1 = _first_max(xg, sub, GROUP_SIZE)
        m2 = jnp.max(jnp.where(sub == f1, neg, xg), axis=0, keepdims=True)
        grp.append(m1 + m2)
    gs = jnp.concatenate(grp, axis=0)
    gidx = _iota((N_GROUPS, tm), 0)
    gsel = gidx < 0
    for _ in range(TOPK_GROUPS):
        _, f = _first_max(gs, gidx, N_GROUPS)
        hit = gidx == f
        gsel = jnp.logical_or(gsel, hit)
        gs = jnp.where(hit, neg, gs)
    eidx = _iota((N_EXPERTS, tm), 0)
    gsel_f = jnp.where(gsel, 1.0, 0.0)
    allowed = jnp.concatenate(
        [jnp.broadcast_to(gsel_f[g:g + 1], (GROUP_SIZE, tm)) for g in range(N_GROUPS)], axis=0) > 0.5
    choice = jnp.where(allowed, biased, neg)
    picks = []
    weights = []
    chosen = jnp.zeros((N_EXPERTS, tm), F32)
    for _ in range(TOP_K):
        _, f = _first_max(choice, eidx, N_EXPERTS)
        hit = eidx == f
        picks.append(f)
        weights.append(jnp.sum(jnp.where(hit, scores, 0.0), axis=0, keepdims=True))
        chosen = jnp.where(hit, 1.0, chosen)
        choice = jnp.where(hit, neg, choice)
    wsum = weights[0]
    for w in weights[1:]:
        wsum = wsum + w
    tr = _iota((tm, tm), 0)
    tc = _iota((tm, tm), 1)
    before = jnp.where(tr < tc, 1.0, 0.0).astype(BF16)
    chosen_b = chosen.astype(BF16)
    rank = _dot(chosen_b, before) + carry[...]
    ranks = [jnp.sum(jnp.where(eidx == f, rank, 0.0), axis=0, keepdims=True) for f in picks]
    e_ref[...] = jnp.concatenate(picks, axis=0)
    rank_ref[...] = jnp.concatenate(ranks, axis=0).astype(I32)
    w_ref[...] = jnp.concatenate(weights, axis=0) / wsum * ROUTE_SCALE
    carry[...] = carry[...] + _dot(chosen_b, jnp.ones((tm, LANES), BF16))
    cnt_ref[...] = carry[...]


def _route(h, w_router_t, bias_col):
    t = h.shape[0]
    tm = ROUTE_TM
    col = lambda: pl.BlockSpec((TOP_K, tm), lambda i: (0, i))
    return pl.pallas_call(
        _route_kernel,
        grid=(t // tm,),
        in_specs=[pl.BlockSpec((tm, D_MODEL), lambda i: (i, 0)),
                  pl.BlockSpec((N_EXPERTS, D_MODEL), lambda i: (0, 0)),
                  pl.BlockSpec((N_EXPERTS, 1), lambda i: (0, 0))],
        out_specs=[col(), col(), col(), pl.BlockSpec((N_EXPERTS, LANES), lambda i: (0, 0))],
        out_shape=[jax.ShapeDtypeStruct((TOP_K, t), I32), jax.ShapeDtypeStruct((TOP_K, t), I32),
                   jax.ShapeDtypeStruct((TOP_K, t), F32), jax.ShapeDtypeStruct((N_EXPERTS, LANES), F32)],
        scratch_shapes=[pltpu.VMEM((N_EXPERTS, LANES), F32)],
        compiler_params=_cparams("arbitrary"),
        name="route",
    )(h, w_router_t, bias_col)


DISPATCH_TM = 128


def _row_copy(src, dst, s_row, d_row, sem):
    return pltpu.make_async_copy(src.at[pl.ds(s_row, 1)], dst.at[pl.ds(d_row, 1)], sem)


def _dispatch_kernel(dest_ref, h_ref, xs_ref, sem):
    def issue(t, _):
        for r in range(TOP_K):
            _row_copy(h_ref, xs_ref, t, dest_ref[t * TOP_K + r], sem).start()
        return 0

    lax.fori_loop(0, DISPATCH_TM, issue, 0)

    def drain(t, _):
        for r in range(TOP_K):
            _row_copy(h_ref, xs_ref, 0, 0, sem).wait()
        return 0

    lax.fori_loop(0, DISPATCH_TM, drain, 0)


def _dispatch(h, dest_flat):
    t = h.shape[0]
    return pl.pallas_call(
        _dispatch_kernel,
        grid=(t // DISPATCH_TM,),
        in_specs=[pl.BlockSpec((DISPATCH_TM * TOP_K,), lambda i: (i,), memory_space=pltpu.SMEM),
                  pl.BlockSpec((DISPATCH_TM, D_MODEL), lambda i: (i, 0))],
        out_specs=pl.BlockSpec(memory_space=pl.ANY),
        out_shape=jax.ShapeDtypeStruct((t * TOP_K, D_MODEL), F32),
        scratch_shapes=[pltpu.SemaphoreType.DMA(())],
        compiler_params=_cparams("arbitrary"),
        name="moe_dispatch",
    )(dest_flat, h)


def _expert_kernel(blk_ref, exp_ref, lo_ref, hi_ref, first_ref, x_ref, wg_ref, wu_ref, wd_ref, y_ref, wgu, wdn):
    del blk_ref
    i = pl.program_id(0)

    @pl.when((i == 0) | (exp_ref[i] != exp_ref[jnp.maximum(i - 1, 0)]))
    def _():
        wgu[:, 0:D_EXPERT] = wg_ref[0].astype(BF16)
        wgu[:, D_EXPERT:] = wu_ref[0].astype(BF16)
        wdn[...] = wd_ref[0].astype(BF16)

    gu = _dot(x_ref[...].astype(BF16), wgu[...])
    gate = gu[:, 0:D_EXPERT]
    act = gate * _sigmoid(gate) * gu[:, D_EXPERT:]
    y = _dot(act.astype(BF16), wdn[...])
    row = _iota((ROW_BLOCK, 1), 0)
    y = jnp.where((row >= lo_ref[i]) & (row < hi_ref[i]), y, 0.0)

    @pl.when(first_ref[i] == 1)
    def _():
        y_ref[...] = y

    @pl.when(first_ref[i] == 0)
    def _():
        y_ref[...] = y_ref[...] + y


def _expert_mm(xs, w_gate, w_up, w_down, blk, exp, lo, hi, first):
    n_items = blk.shape[0]
    grid_spec = pltpu.PrefetchScalarGridSpec(
        num_scalar_prefetch=5,
        grid=(n_items,),
        in_specs=[pl.BlockSpec((ROW_BLOCK, D_MODEL), lambda i, blk, exp, lo, hi, fi: (blk[i], 0)),
                  pl.BlockSpec((1, D_MODEL, D_EXPERT), lambda i, blk, exp, lo, hi, fi: (exp[i], 0, 0)),
                  pl.BlockSpec((1, D_MODEL, D_EXPERT), lambda i, blk, exp, lo, hi, fi: (exp[i], 0, 0)),
                  pl.BlockSpec((1, D_EXPERT, D_MODEL), lambda i, blk, exp, lo, hi, fi: (exp[i], 0, 0))],
        out_specs=pl.BlockSpec((ROW_BLOCK, D_MODEL), lambda i, blk, exp, lo, hi, fi: (blk[i], 0)),
        scratch_shapes=[pltpu.VMEM((D_MODEL, 2 * D_EXPERT), BF16), pltpu.VMEM((D_EXPERT, D_MODEL), BF16)],
    )
    return pl.pallas_call(
        _expert_kernel,
        grid_spec=grid_spec,
        out_shape=jax.ShapeDtypeStruct(xs.shape, F32),
        compiler_params=_cparams("arbitrary"),
        name="moe_experts",
    )(blk, exp, lo, hi, first, xs, w_gate, w_up, w_down)


COMBINE_TM = 128


def _combine_kernel(dest_ref, next_ref, h_ref, wt_ref, y_ref, wsg_ref, wsd_ref, g_ref, b_ref, o_ref, buf, sem):
    i = pl.program_id(0)
    slot = i % 2

    def gather(d_ref, s):
        def issue(t, _):
            for r in range(TOP_K):
                pltpu.make_async_copy(y_ref.at[pl.ds(d_ref[t * TOP_K + r], 1)], buf.at[s, r, pl.ds(t, 1)],
                                      sem.at[s]).start()
            return 0

        lax.fori_loop(0, COMBINE_TM, issue, 0)

    @pl.when(i == 0)
    def _():
        gather(dest_ref, 0)

    @pl.when(i + 1 < pl.num_programs(0))
    def _():
        gather(next_ref, 1 - slot)

    h = h_ref[...]
    gu = _dot(h.astype(BF16), wsg_ref[...])
    gate = gu[:, 0:D_SHARED]
    f = _dot((gate * _sigmoid(gate) * gu[:, D_SHARED:]).astype(BF16), wsd_ref[...])

    def drain(t, _):
        for r in range(TOP_K):
            pltpu.make_async_copy(y_ref.at[pl.ds(0, 1)], buf.at[slot, r, pl.ds(t, 1)], sem.at[slot]).wait()
        return 0

    lax.fori_loop(0, COMBINE_TM, drain, 0)
    for r in range(TOP_K):
        f = f + wt_ref[:, r:r + 1] * buf[slot, r]
    o_ref[...] = _layer_norm(DN_ALPHA * h + f, g_ref[...], b_ref[...])


def _combine(h, w_tok, dest_flat, y, w_sgu, w_sd, g, b):
    t = h.shape[0]
    tm = COMBINE_TM
    n = t // tm
    return pl.pallas_call(
        _combine_kernel,
        grid=(n,),
        in_specs=[pl.BlockSpec((tm * TOP_K,), lambda i: (i,), memory_space=pltpu.SMEM),
                  pl.BlockSpec((tm * TOP_K,), lambda i: (jnp.minimum(i + 1, n - 1),), memory_space=pltpu.SMEM),
                  pl.BlockSpec((tm, D_MODEL), lambda i: (i, 0)),
                  pl.BlockSpec((tm, TOP_K), lambda i: (i, 0)),
                  pl.BlockSpec(memory_space=pl.ANY),
                  _const_spec((D_MODEL, 2 * D_SHARED)), _const_spec((D_SHARED, D_MODEL)),
                  _const_spec((1, D_MODEL)), _const_spec((1, D_MODEL))],
        out_specs=pl.BlockSpec((tm, D_MODEL), lambda i: (i, 0)),
        out_shape=jax.ShapeDtypeStruct((t, D_MODEL), F32),
        scratch_shapes=[pltpu.VMEM((2, TOP_K, tm, D_MODEL), F32), pltpu.SemaphoreType.DMA((2,))],
        compiler_params=_cparams("arbitrary"),
        name="moe_combine",
    )(dest_flat, dest_flat, h, w_tok, y, w_sgu, w_sd, g, b)


def _moe_plan(counts, n_rows):
    n_blocks = n_rows // ROW_BLOCK
    n_items = n_blocks + N_EXPERTS
    start = jnp.cumsum(counts) - counts
    end = start + counts
    first_blk = start // ROW_BLOCK
    last_blk = jnp.maximum(end - 1, start) // ROW_BLOCK
    nb = jnp.where(counts > 0, last_blk - first_blk + 1, 0)
    item_end = jnp.cumsum(nb)
    item_start = item_end - nb
    total = item_end[-1]
    idx = jnp.arange(n_items, dtype=I32)
    last_item = jnp.maximum(total - 1, 0)
    src = jnp.minimum(idx, last_item)
    exp = jnp.minimum(jnp.sum(item_end[None, :] <= src[:, None], axis=1), N_EXPERTS - 1).astype(I32)
    onehot = exp[:, None] == jnp.arange(N_EXPERTS, dtype=I32)[None, :]
    take = lambda table: jnp.sum(jnp.where(onehot, table[None, :], 0), axis=1)
    blk = (take(first_blk) + src - take(item_start)).astype(I32)
    valid = idx < total
    lo = jnp.clip(take(start) - blk * ROW_BLOCK, 0, ROW_BLOCK)
    hi = jnp.clip(take(end) - blk * ROW_BLOCK, 0, ROW_BLOCK)
    first = (valid & (lo == 0)).astype(I32)
    lo = jnp.where(valid, lo, 0).astype(I32)
    hi = jnp.where(valid, hi, 0).astype(I32)
    return start, blk, exp, lo, hi, first


def _moe(h, w_router, router_bias, w_exp_gate, w_exp_up, w_exp_down, w_sgu, w_sd, g, b):
    t = h.shape[0]
    picks, ranks, weights, cnt = _route(h, w_router.T, router_bias.astype(F32).reshape(N_EXPERTS, 1))
    counts = cnt[:, 0].astype(I32)
    start, blk, exp, lo, hi, first = _moe_plan(counts, t * TOP_K)
    experts = jnp.arange(N_EXPERTS, dtype=I32)[:, None, None]
    dest = ranks + jnp.sum(jnp.where(picks[None] == experts, start.astype(I32)[:, None, None], 0), axis=0)
    dest_flat = dest.T.reshape(t * TOP_K)
    xs = _dispatch(h, dest_flat)
    y = _expert_mm(xs, w_exp_gate, w_exp_up, w_exp_down, blk, exp, lo, hi, first)
    return _combine(h, weights.T, dest_flat, y, w_sgu, w_sd, g, b)


def kernel(x_prompt, x_sample, cache_sb_k, cache_sb_v, page_table, state_gdn, state_gdn_conv, cache_mem_k, cache_mem_v, mem_prompt, w_in, sb_logit_bias, conv_w, gdn_A_log, gdn_dt_bias, gdn_norm_w, w_mem_k, w_mem_v, w_br_sb, w_br_gdn, w_br_mem, w_o, ln1_g, ln1_b, w_router, router_bias, w_exp_gate, w_exp_up, w_exp_down, w_sh_gate, w_sh_up, w_sh_down, ln2_g, ln2_b):
    bsz, seq, _ = x_prompt.shape
    db = x_sample.shape[0]
    tp = bsz * seq

    ba_off = 3 * SB_W + GDN_CONV_W + GDN_V_W
    mq_off = ba_off + 2 * GDN_HEADS
    gate_off = mq_off + MEM_W
    w_main = jnp.concatenate([w_in[:, :ba_off], w_in[:, mq_off:gate_off]], axis=1).astype(BF16)
    w_ba = jnp.pad(w_in[:, ba_off:mq_off], ((0, 0), (0, LANES - 2 * GDN_HEADS))).astype(BF16)
    w_gates = w_in[:, gate_off:].astype(BF16)
    w_kv = jnp.concatenate([w_mem_k, w_mem_v], axis=1).astype(BF16)
    w_sb, w_gdn, w_mem, w_ob = (w.astype(BF16) for w in (w_br_sb, w_br_gdn, w_br_mem, w_o))
    w_sgu = jnp.concatenate([w_sh_gate, w_sh_up], axis=1).astype(BF16)
    w_sd = w_sh_down.astype(BF16)
    row = lambda v: v.astype(F32).reshape(1, -1)
    head_lane = lambda v: jnp.pad(v.astype(F32), (GDN_HEADS, LANES - 2 * GDN_HEADS)).reshape(1, LANES)
    nw_pair = jnp.tile(gdn_norm_w.astype(F32), 2).reshape(1, LANES)
    bias = sb_logit_bias.astype(F32)

    xp = x_prompt.reshape(tp, D_MODEL)
    q, k, v, gqkv, z, mq, ba = _proj_in(xp, w_main, w_ba, 512)
    mem_k, mem_v = _mem_kv(mem_prompt.reshape(bsz * N_MEM, D_MODEL), w_kv, 512)
    sh = lambda a: a.reshape(bsz, seq, a.shape[-1])
    o_sb = _sb_prompt(sh(q), sh(k), sh(v), bias)
    gq3 = sh(gqkv)
    o_gdn, s_p = _gdn_prompt(gq3, sh(z), sh(ba), conv_w.astype(F32), head_lane(gdn_A_log),
                             head_lane(gdn_dt_bias), nw_pair)
    o_mem = _mem_prompt(sh(mq), mem_k.reshape(bsz, N_MEM, MEM_W), mem_v.reshape(bsz, N_MEM, MEM_W), min(512, seq))
    h_p = _merge_ln(xp, o_sb.reshape(tp, SB_W), o_gdn.reshape(tp, GDN_V_W), o_mem.reshape(tp, MEM_W),
                    w_gates, w_sb, w_gdn, w_mem, w_ob, row(ln1_g), row(ln1_b), 512)

    xs = x_sample.reshape(db, D_MODEL)
    q_s, k_s, v_s, gqkv_s, z_s, mq_s, ba_s = _proj_in(xs, w_main, w_ba, db)
    o_sb_s = _sb_sample(q_s, cache_sb_k, cache_sb_v, page_table, bias)
    o_gdn_s, s_s = _gdn_sample(gqkv_s, state_gdn_conv.astype(F32), ba_s, z_s, state_gdn.astype(F32),
                               conv_w.astype(F32), gdn_A_log, gdn_dt_bias, gdn_norm_w)
    o_mem_s = _mem_sample(mq_s, cache_mem_k, cache_mem_v)
    h_s = _merge_ln(xs, o_sb_s, o_gdn_s, o_mem_s, w_gates, w_sb, w_gdn, w_mem, w_ob, row(ln1_g), row(ln1_b), db)

    y = _moe(jnp.concatenate([h_p, h_s], axis=0), w_router, router_bias, w_exp_gate, w_exp_up, w_exp_down,
             w_sgu, w_sd, row(ln2_g), row(ln2_b))

    heads = lambda a, n, hd: a.reshape(n, -1, hd[0], hd[1])
    return (y[:tp].reshape(bsz, seq, D_MODEL), y[tp:].reshape(db, 1, D_MODEL),
            heads(k, bsz, (SB_HEADS, SB_HEAD_DIM)), heads(v, bsz, (SB_HEADS, SB_HEAD_DIM)),
            s_p, gq3[:, seq - (GDN_CONV - 1):, :],
            mem_k.reshape(bsz, N_MEM, MEM_HEADS, MEM_HEAD_DIM), mem_v.reshape(bsz, N_MEM, MEM_HEADS, MEM_HEAD_DIM),
            heads(k_s, db, (SB_HEADS, SB_HEAD_DIM)), heads(v_s, db, (SB_HEADS, SB_HEAD_DIM)),
            s_s, jnp.concatenate([state_gdn_conv[:, 1:, :], gqkv_s[:, None, :]], axis=1))
```

```python
import functools

import jax
import jax.numpy as jnp
from jax import lax
from jax.experimental import pallas as pl
from jax.experimental.pallas import tpu as pltpu

F32 = jnp.float32
BF16 = jnp.bfloat16
I32 = jnp.int32

D_MODEL = 1024
PAGE_SIZE = 128
SB_HEADS = 8
SB_HEAD_DIM = 64
SB_W = SB_HEADS * SB_HEAD_DIM
GDN_HEADS = 8
GDN_K_DIM = 64
GDN_V_DIM = 64
GDN_CONV = 4
GDN_QK_W = GDN_HEADS * GDN_K_DIM
GDN_V_W = GDN_HEADS * GDN_V_DIM
GDN_CONV_W = 2 * GDN_QK_W + GDN_V_W
N_MEM = 256
MEM_HEADS = 4
MEM_HEAD_DIM = 128
MEM_W = MEM_HEADS * MEM_HEAD_DIM
N_EXPERTS = 64
TOP_K = 8
N_GROUPS = 8
TOPK_GROUPS = 4
D_EXPERT = 256
D_SHARED = 256
ROUTE_SCALE = 2.5
DEPTH = 1
DN_ALPHA = (2 * DEPTH) ** 0.25
LN_EPS = 1e-5
RMS_EPS = 1e-6
L2_EPS = 1e-6

LANES = 128
CHUNK = 128
ROW_BLOCK = 256
VMEM_LIMIT = 56 * 1024 * 1024


def _cparams(*sem):
    return pltpu.CompilerParams(dimension_semantics=sem, vmem_limit_bytes=VMEM_LIMIT)


def _dot(a, b):
    return jnp.dot(a, b, preferred_element_type=F32)


def _dot_nt(a, b):
    return lax.dot_general(a, b, (((1,), (1,)), ((), ())), preferred_element_type=F32)


def _split2(x):
    hi = x.astype(BF16)
    lo = (x - hi.astype(F32)).astype(BF16)
    return hi, lo


def _split3(x):
    hi = x.astype(BF16)
    r = x - hi.astype(F32)
    mid = r.astype(BF16)
    lo = (r - mid.astype(F32)).astype(BF16)
    return hi, mid, lo


def _dot_xc(x, m):
    return _dot(jnp.concatenate(_split3(x), axis=1), jnp.concatenate([m, m, m], axis=0))


def _dot_cx(m, x):
    return _dot(jnp.concatenate([m, m, m], axis=1), jnp.concatenate(_split3(x), axis=0))


def _dot_ff(a, b):
    ah, al = _split2(a)
    bh, bl = _split2(b)
    return _dot(jnp.concatenate([ah, ah, al], axis=1), jnp.concatenate([bh, bl, bh], axis=0))


def _sigmoid(x):
    return 1.0 / (1.0 + jnp.exp(-x))


def _softplus(x):
    return jnp.maximum(x, 0.0) + jnp.log(1.0 + jnp.exp(-jnp.abs(x)))


def _iota(shape, dim):
    return lax.broadcasted_iota(I32, shape, dim)


def _const_spec(shape):
    nd = len(shape)
    return pl.BlockSpec(shape, lambda *_: (0,) * nd, pipeline_mode=pl.Buffered(1))


def _layer_norm(x, g, b):
    mu = jnp.mean(x, axis=-1, keepdims=True)
    xc = x - mu
    var = jnp.mean(xc * xc, axis=-1, keepdims=True)
    return xc * lax.rsqrt(var + LN_EPS) * g + b


def _proj_kernel(x_ref, w_ref, wkv_ref, wba_ref, q_ref, kt_ref, vt_ref, g_ref, z_ref, mq_ref, ba_ref):
    xb = x_ref[...].astype(BF16)
    q_ref[...] = (_dot(xb, w_ref[:, 0:512]) * 0.125).astype(BF16)
    kvt = _dot_nt(wkv_ref[...], xb)
    kt_ref[0] = kvt[0:SB_W]
    vt_ref[0] = kvt[SB_W:]
    for c in range(3):
        g_ref[:, c * 512:(c + 1) * 512] = _dot(xb, w_ref[:, 512 + c * 512:1024 + c * 512])
    z_ref[...] = _dot(xb, w_ref[:, 2048:2560]).astype(BF16)
    mq_ref[...] = _dot(xb, w_ref[:, 2560:3072]).astype(BF16)
    ba_ref[...] = _dot(xb, wba_ref[...])


def _proj_in(x2d, seq, w_main, w_kvt, w_ba, tm):
    t = x2d.shape[0]
    per_seq = seq // tm
    row = lambda w: pl.BlockSpec((tm, w), lambda i: (i, 0))
    tr = lambda: pl.BlockSpec((1, SB_W, tm), lambda i: (i // per_seq, 0, i % per_seq))
    return pl.pallas_call(
        _proj_kernel,
        grid=(t // tm,),
        in_specs=[row(D_MODEL), _const_spec((D_MODEL, 3072)), _const_spec((2 * SB_W, D_MODEL)),
                  _const_spec((D_MODEL, LANES))],
        out_specs=[row(512), tr(), tr(), row(1536), row(512), row(512), row(LANES)],
        out_shape=[jax.ShapeDtypeStruct((t, 512), BF16), jax.ShapeDtypeStruct((t // seq, SB_W, seq), F32),
                   jax.ShapeDtypeStruct((t // seq, SB_W, seq), F32), jax.ShapeDtypeStruct((t, 1536), F32),
                   jax.ShapeDtypeStruct((t, 512), BF16), jax.ShapeDtypeStruct((t, 512), BF16),
                   jax.ShapeDtypeStruct((t, LANES), F32)],
        compiler_params=_cparams("parallel"),
        name="proj_in",
    )(x2d, w_main, w_kvt, w_ba)


def _memkv_kernel(x_ref, w_ref, k_ref, v_ref):
    xb = x_ref[...].astype(BF16)
    k_ref[...] = _dot(xb, w_ref[:, 0:512])
    v_ref[...] = _dot(xb, w_ref[:, 512:1024])


def _mem_kv(x2d, w_kv, tm):
    t = x2d.shape[0]
    row = lambda w: pl.BlockSpec((tm, w), lambda i: (i, 0))
    return pl.pallas_call(
        _memkv_kernel,
        grid=(t // tm,),
        in_specs=[row(D_MODEL), _const_spec((D_MODEL, 1024))],
        out_specs=[row(512), row(512)],
        out_shape=[jax.ShapeDtypeStruct((t, 512), F32)] * 2,
        compiler_params=_cparams("parallel"),
        name="mem_kv",
    )(x2d, w_kv)


def _cumsum_matrix():
    r = _iota((2 * CHUNK, 2 * CHUNK), 0) % CHUNK
    c = _iota((2 * CHUNK, 2 * CHUNK), 1)
    return jnp.where((c >= CHUNK) | (r > c), 1.0, 0.0).astype(BF16)


SB_QT = 512


SB_KB = 2


def _sb_prompt_kernel(qt, bias_ref, q_ref, k_ref, v_ref, o_ref, kms, vms, carry, acc):
    hp = pl.program_id(1)
    seq = q_ref.shape[1]
    nblk = seq // CHUNK
    span = SB_KB * CHUNK
    first = _iota((LANES, CHUNK), 0) < SB_HEAD_DIM
    tt = _cumsum_matrix()
    dist = _iota((qt, LANES), 0) - _iota((qt, LANES), 1)
    biases = (bias_ref[2 * hp], bias_ref[2 * hp + 1])

    def stage(i, _):
        cols = pl.ds(pl.multiple_of(i * CHUNK, CHUNK), CHUNK)
        base = pl.multiple_of(i * 2 * CHUNK, 2 * CHUNK)
        for src, dst in ((k_ref, kms), (v_ref, vms)):
            x = src[0, :, cols]
            dst[:, pl.ds(base, CHUNK)] = jnp.where(first, x, 0.0).astype(BF16)
            dst[:, pl.ds(base + CHUNK, CHUNK)] = jnp.where(first, 0.0, x).astype(BF16)
        zero = jnp.zeros((CHUNK, LANES), F32)
        carry[0, cols, :] = zero
        carry[1, cols, :] = zero
        acc[cols, :] = zero
        return 0

    lax.fori_loop(0, nblk, stage, 0)

    def kstep(pp, _):
        p = nblk // SB_KB - 1 - pp
        key0 = p * span
        staged = pl.ds(pl.multiple_of(p * 2 * span, 2 * span), 2 * span)
        km = kms[:, staged]
        vm = vms[:, staged]

        def qtile(t, _):
            row0 = pl.multiple_of(t * qt, qt)
            rows = pl.ds(row0, qt)
            z_all = _dot(q_ref[0, rows, :], km)
            zs, sps, parts, masks = [], [], [], []
            for g in range(2 * SB_KB):
                causal = dist > key0 + (g // 2) * CHUNK - row0
                z = z_all[:, g * CHUNK:(g + 1) * CHUNK] + biases[g % 2]
                sp = _softplus(z)
                hi, lo = _split2(jnp.where(causal, sp, 0.0))
                zs.append(z)
                sps.append(sp)
                masks.append(causal)
                parts.append(jnp.concatenate([hi, lo], axis=1))
            r_all = _dot(jnp.concatenate(parts, axis=0), tt)
            a = [None] * (2 * SB_KB)
            for h in range(2):
                c = carry[h, rows, :]
                for blk in reversed(range(SB_KB)):
                    g = 2 * blk + h
                    r = r_all[g * qt:(g + 1) * qt]
                    e = jnp.exp(zs[g] - sps[g] - r[:, :CHUNK] - c)
                    a[g] = jnp.where(masks[g], e, 0.0).astype(BF16)
                    c = c + r[:, CHUNK:]
                carry[h, rows, :] = c
            acc[rows, :] = acc[rows, :] + _dot_nt(jnp.concatenate(a, axis=1), vm)
            return 0

        lax.fori_loop(key0 // qt, seq // qt, qtile, 0)
        return 0

    lax.fori_loop(0, nblk // SB_KB, kstep, 0)

    def write_out(i, _):
        rows = pl.ds(pl.multiple_of(i * CHUNK, CHUNK), CHUNK)
        o_ref[0, rows, :] = acc[rows, :].astype(BF16)
        return 0

    lax.fori_loop(0, nblk, write_out, 0)


def _sb_prompt(q, kt, vt, bias):
    b, seq, _ = q.shape
    qt = min(SB_QT, seq)
    blk = lambda: pl.BlockSpec((1, seq, LANES), lambda i, j: (i, 0, j))
    blk_t = lambda: pl.BlockSpec((1, LANES, seq), lambda i, j: (i, j, 0))
    return pl.pallas_call(
        functools.partial(_sb_prompt_kernel, qt),
        grid=(b, SB_W // LANES),
        in_specs=[pl.BlockSpec(memory_space=pltpu.SMEM), blk(), blk_t(), blk_t()],
        out_specs=blk(),
        out_shape=jax.ShapeDtypeStruct((b, seq, SB_W), BF16),
        scratch_shapes=[pltpu.VMEM((LANES, 2 * seq), BF16), pltpu.VMEM((LANES, 2 * seq), BF16),
                        pltpu.VMEM((2, seq, LANES), F32), pltpu.VMEM((seq, LANES), F32)],
        compiler_params=_cparams("parallel", "parallel"),
        name="sb_prompt",
    )(bias, q, kt, vt)


def _sb_sample_kernel(n_pages, pt_ref, q_ref, bias_ref, *refs):
    del pt_ref
    k_refs = refs[:n_pages]
    v_refs = refs[n_pages:2 * n_pages]
    o_ref = refs[2 * n_pages]
    head_of_lane = _iota((SB_HEADS, SB_W), 1) // SB_HEAD_DIM
    own = head_of_lane == _iota((SB_HEADS, SB_W), 0)
    qm = jnp.where(own, q_ref[0].astype(F32), 0.0).astype(BF16)
    pages = range(n_pages)
    kcat = jnp.concatenate([k_refs[p][0].astype(BF16) for p in pages], axis=1)
    z = _dot(qm, kcat)
    zr = jnp.concatenate([z[:, p * PAGE_SIZE:(p + 1) * PAGE_SIZE] + bias_ref[...] for p in pages], axis=0)
    sp = _softplus(zr)
    hi, lo = _split2(sp)
    r = _dot(jnp.concatenate([hi, lo], axis=1), _cumsum_matrix())
    carries = [None] * n_pages
    c = jnp.zeros((SB_HEADS, LANES), F32)
    for p in reversed(pages):
        carries[p] = c
        c = c + r[p * SB_HEADS:(p + 1) * SB_HEADS, PAGE_SIZE:]
    a = jnp.exp(zr - sp - r[:, :PAGE_SIZE] - jnp.concatenate(carries, axis=0))
    acat = jnp.concatenate([a[p * SB_HEADS:(p + 1) * SB_HEADS] for p in pages], axis=1).astype(BF16)
    vcat = jnp.concatenate([v_refs[p][0].astype(BF16) for p in pages], axis=1)
    o = _dot_nt(acat, vcat)
    o_ref[0] = jnp.sum(jnp.where(own, o, 0.0), axis=0, keepdims=True).astype(BF16)


def _sb_sample(q, cache_k, cache_v, page_table, bias):
    db, n_pages = page_table.shape
    n_pool = cache_k.shape[0]
    ck = cache_k.transpose(0, 2, 3, 1).reshape(n_pool, SB_W, PAGE_SIZE)
    cv = cache_v.transpose(0, 2, 3, 1).reshape(n_pool, SB_W, PAGE_SIZE)
    bias2d = jnp.broadcast_to(bias.astype(F32)[:, None], (SB_HEADS, LANES))

    def page_spec(p):
        return pl.BlockSpec((1, SB_W, PAGE_SIZE), lambda i, pt: (pt[i, p], 0, 0))

    grid_spec = pltpu.PrefetchScalarGridSpec(
        num_scalar_prefetch=1,
        grid=(db,),
        in_specs=[pl.BlockSpec((1, 1, SB_W), lambda i, pt: (i, 0, 0)),
                  pl.BlockSpec((SB_HEADS, LANES), lambda i, pt: (0, 0))]
        + [page_spec(p) for p in range(n_pages)] * 2,
        out_specs=pl.BlockSpec((1, 1, SB_W), lambda i, pt: (i, 0, 0)),
    )
    out = pl.pallas_call(
        functools.partial(_sb_sample_kernel, n_pages),
        grid_spec=grid_spec,
        out_shape=jax.ShapeDtypeStruct((db, 1, SB_W), BF16),
        compiler_params=_cparams("parallel"),
        name="sb_sample",
    )(page_table, q.reshape(db, 1, SB_W), bias2d, *([ck] * n_pages), *([cv] * n_pages))
    return out.reshape(db, SB_W)


def _head_ones():
    r = _iota((LANES, LANES), 0) // GDN_K_DIM
    c = _iota((LANES, LANES), 1) // GDN_K_DIM
    return jnp.where(r == c, 1.0, 0.0).astype(BF16)


def _unit_lower_solve(a, rhs):
    p = -a
    y = rhs
    for _ in range(6):
        r = _dot_ff(p, jnp.concatenate([p, y], axis=1))
        y = y + r[:, CHUNK:]
        p = r[:, :CHUNK]
    return y + _dot_ff(p, y)


GDN_TILE = 512


def _gdn_prompt_parts(tile, hp, qp_ref, kp_ref, vp_ref, z_ref, ba_ref, cwq_ref, cwk_ref, cwv_ref, alog_ref, dtb_ref,
                      nw_ref, o_ref, s_ref, qs, ks, vs, exs, os_):
    seq = qp_ref.shape[1]
    pad = 8
    row = _iota((CHUNK, CHUNK), 0)
    col = _iota((CHUNK, CHUNK), 1)
    lower = row >= col
    strict = row > col
    first = col < GDN_K_DIM
    same_head = (row // GDN_K_DIM) == (col // GDN_K_DIM)
    head_ones = _head_ones()
    lower_ones = jnp.where(lower, 1.0, 0.0).astype(BF16)
    er = _iota((LANES, 4 * LANES), 0)
    ec = _iota((LANES, 4 * LANES), 1)
    sec = ec // LANES
    pair_head = 2 * hp + (ec % LANES) // GDN_K_DIM
    want = jnp.where(sec == 0, pair_head,
                     jnp.where(sec == 1, GDN_HEADS + pair_head,
                               jnp.where(sec == 2, GDN_HEADS + 2 * hp, GDN_HEADS + 2 * hp + 1)))
    expand = jnp.where(er == want, 1.0, 0.0).astype(BF16)
    lane_row = _iota((1, LANES), 1)
    neg_a = -jnp.exp(alog_ref[...])
    scale = GDN_K_DIM ** -0.5

    n_sub = tile // CHUNK

    def conv(scr, cw_ref, r0):
        w = scr[pl.ds(r0, tile + pad), :]
        y = (w[5:5 + tile] * cw_ref[0:1, :] + w[6:6 + tile] * cw_ref[1:2, :]
             + w[7:7 + tile] * cw_ref[2:3, :] + w[8:8 + tile] * cw_ref[3:4, :])
        return y * _sigmoid(y)

    def prepare(tt, _):
        r0 = pl.multiple_of((seq // tile - 1 - tt) * tile, tile)
        rows = pl.ds(r0, tile)
        q = conv(qs, cwq_ref, r0)
        k = conv(ks, cwk_ref, r0)
        v = conv(vs, cwv_ref, r0)
        ss = _dot_xc(jnp.concatenate([q * q, k * k], axis=0), head_ones)
        qs[pl.ds(r0 + pad, tile), :] = q * lax.rsqrt(ss[:tile] + L2_EPS) * scale
        ks[pl.ds(r0 + pad, tile), :] = k * lax.rsqrt(ss[tile:] + L2_EPS)
        vs[pl.ds(r0 + pad, tile), :] = v
        raw = ba_ref[0, rows, :]
        beta_raw = _sigmoid(raw)
        g_raw = neg_a * _softplus(raw + dtb_ref[...])
        g_wide = jnp.concatenate([g_raw[i * CHUNK:(i + 1) * CHUNK] for i in range(n_sub)], axis=1)
        c_wide = _dot_cx(lower_ones, g_wide)
        gcum = jnp.concatenate([c_wide[:, i * LANES:(i + 1) * LANES] for i in range(n_sub)], axis=0)
        exs[rows, :] = _dot_xc(jnp.where(lane_row < GDN_HEADS, beta_raw, gcum), expand)
        return 0

    def setup():
        for scr, src in ((qs, qp_ref), (ks, kp_ref), (vs, vp_ref)):
            scr[0:pad, :] = jnp.zeros((pad, LANES), F32)
            scr[pad:seq + pad, :] = src[0]
        lax.fori_loop(0, seq // tile, prepare, 0)

    def within_chunk(c):
        r0 = pl.multiple_of(c * CHUNK, CHUNK)
        rows = pl.ds(r0, CHUNK)
        srows = pl.ds(r0 + pad, CHUNK)
        qn = qs[srows, :]
        kn = ks[srows, :]
        v = vs[srows, :]
        ex = exs[rows, :]
        beta_p = ex[:, 0:LANES]
        gc_p = ex[:, LANES:2 * LANES]
        eg_p = jnp.exp(gc_p)
        g_last = gc_p[CHUNK - 1:CHUNK, :]
        kb = kn * beta_p
        lhs = jnp.concatenate([jnp.where(first, kb, 0.0), jnp.where(first, qn, 0.0),
                               jnp.where(first, 0.0, kb), jnp.where(first, 0.0, qn)], axis=0).astype(BF16)
        kq = _dot_nt(lhs, kn.astype(BF16))
        rhs = jnp.concatenate([v * beta_p, kb * eg_p], axis=1)
        sols = []
        qks = []
        for h in range(2):
            g1 = ex[:, (2 + h) * LANES:(3 + h) * LANES]
            diff = g1 - g1.T
            dec = jnp.where(lower, jnp.exp(jnp.where(lower, diff, 0.0)), 0.0)
            kk = kq[2 * h * CHUNK:(2 * h + 1) * CHUNK]
            sols.append(_unit_lower_solve(jnp.where(strict, kk * dec, 0.0), rhs))
            qks.append((kq[(2 * h + 1) * CHUNK:(2 * h + 2) * CHUNK] * dec).astype(BF16))
        first2 = jnp.concatenate([first, first], axis=1)
        sol = jnp.where(first2, sols[0], sols[1])
        u = sol[:, 0:LANES]
        wq = jnp.concatenate([sol[:, LANES:2 * LANES], qn * eg_p], axis=0).astype(BF16)
        k_tail_t = (kn * jnp.exp(g_last - gc_p)).T.astype(BF16)
        return rows, u, wq, jnp.concatenate(qks, axis=1), k_tail_t, jnp.exp(g_last)

    def advance(s, parts):
        rows, u, wq, qk2, k_tail_t, decay = parts
        ws = _dot(wq, s.astype(BF16))
        v_new = u - ws[:CHUNK]
        v_heads = jnp.concatenate([jnp.where(first, v_new, 0.0), jnp.where(first, 0.0, v_new)], axis=0)
        os_[rows, :] = ws[CHUNK:] + _dot(qk2, v_heads.astype(BF16))
        upd = _dot(k_tail_t, v_new.astype(BF16))
        return s * decay + jnp.where(same_head, upd, 0.0)

    def normalize(tt, _):
        rows = pl.ds(pl.multiple_of(tt * tile, tile), tile)
        o = os_[rows, :]
        ms = _dot_xc(o * o, head_ones) * (1.0 / GDN_V_DIM)
        zf = z_ref[0, rows, :].astype(F32)
        o_ref[0, rows, :] = (o * lax.rsqrt(ms + RMS_EPS) * nw_ref[...] * (zf * _sigmoid(zf))).astype(BF16)
        return 0

    def finish(s):
        s_ref[0, 0] = s[0:GDN_K_DIM, 0:GDN_V_DIM]
        s_ref[0, 1] = s[GDN_K_DIM:, GDN_V_DIM:]
        lax.fori_loop(0, seq // tile, normalize, 0)

    return setup, within_chunk, advance, finish


GDN_GROUP = 4


def _gdn_prompt_kernel(tile, group, *refs):
    setup, within_chunk, advance, finish = _gdn_prompt_parts(tile, pl.program_id(1), *refs)
    seq = refs[0].shape[1]
    setup()

    def chunk_group(cg, s):
        parts = [within_chunk(cg * group + i) for i in range(group)]
        for p in parts:
            s = advance(s, p)
        return s

    finish(lax.fori_loop(0, seq // (CHUNK * group), chunk_group, jnp.zeros((LANES, LANES), F32)))


def _gdn_prompt(gqkv, z, ba, conv_w, alog_lane, dtb_lane, nw_lane):
    b, seq, _ = gqkv.shape
    n_pair = GDN_HEADS // 2
    blk = lambda off: pl.BlockSpec((1, seq, LANES), lambda i, j: (i, 0, off + j))
    cw = lambda off: pl.BlockSpec((GDN_CONV, LANES), lambda i, j: (0, off + j))
    lane_vec = lambda: pl.BlockSpec((1, LANES), lambda i, j: (0, 0))
    return pl.pallas_call(
        functools.partial(_gdn_prompt_kernel, min(GDN_TILE, seq), min(GDN_GROUP, seq // CHUNK)),
        grid=(b, n_pair),
        in_specs=[blk(0), blk(n_pair), blk(2 * n_pair), blk(0),
                  pl.BlockSpec((1, seq, LANES), lambda i, j: (i, 0, 0)),
                  cw(0), cw(n_pair), cw(2 * n_pair), lane_vec(), lane_vec(), lane_vec()],
        out_specs=[blk(0), pl.BlockSpec((1, 2, GDN_K_DIM, GDN_V_DIM), lambda i, j: (i, j, 0, 0))],
        out_shape=[jax.ShapeDtypeStruct((b, seq, GDN_V_W), BF16),
                   jax.ShapeDtypeStruct((b, GDN_HEADS, GDN_K_DIM, GDN_V_DIM), F32)],
        scratch_shapes=[pltpu.VMEM((seq + 8, LANES), F32)] * 3
        + [pltpu.VMEM((seq, 4 * LANES), F32), pltpu.VMEM((seq, LANES), F32)],
        compiler_params=_cparams("parallel", "parallel"),
        name="gdn_prompt",
    )(gqkv, gqkv, gqkv, z, ba, conv_w, conv_w, conv_w, alog_lane, dtb_lane, nw_lane)


GDN_BT = 8


def _gdn_sample_kernel(g_ref, c0_ref, gt_ref, c0t_ref, ba_ref, bat_ref, z_ref, s_ref, cw_ref, cwt_ref,
                       alogc_ref, dtbc_ref, nw_ref, o_ref, sn_ref, o_scr):
    scale = GDN_K_DIM ** -0.5
    y = (c0_ref[:, 0, :] * cw_ref[0:1, :] + c0_ref[:, 1, :] * cw_ref[1:2, :]
         + c0_ref[:, 2, :] * cw_ref[2:3, :] + g_ref[...] * cw_ref[3:4, :])
    v_row = (y * _sigmoid(y))[:, 2 * GDN_QK_W:]
    raw = ba_ref[...]
    er = _iota((LANES, GDN_V_W), 0)
    ec = _iota((LANES, GDN_V_W), 1)
    expand = jnp.where(er == ec // GDN_V_DIM, 1.0, 0.0).astype(BF16)
    beta_row = _dot_xc(_sigmoid(raw), expand)
    u_row = v_row * beta_row
    yt = (c0t_ref[0, 0] * cwt_ref[0] + c0t_ref[0, 1] * cwt_ref[1] + c0t_ref[0, 2] * cwt_ref[2]
          + gt_ref[0] * cwt_ref[3])
    yt = yt * _sigmoid(yt)
    rawt = bat_ref[0]
    beta_c = _sigmoid(rawt[0:GDN_HEADS])
    g_c = -jnp.exp(alogc_ref[...]) * _softplus(rawt[GDN_HEADS:] + dtbc_ref[...])
    eg_c = jnp.exp(g_c)
    for h in range(GDN_HEADS):
        qh = yt[h * GDN_K_DIM:(h + 1) * GDN_K_DIM]
        kh = yt[GDN_QK_W + h * GDN_K_DIM:GDN_QK_W + (h + 1) * GDN_K_DIM]
        qn = qh * lax.rsqrt(jnp.sum(qh * qh, axis=0, keepdims=True) + L2_EPS) * scale
        kn = kh * lax.rsqrt(jnp.sum(kh * kh, axis=0, keepdims=True) + L2_EPS)
        eg = eg_c[h:h + 1, :]
        wn = kn * (beta_c[h:h + 1, :] * eg)
        qd = qn * eg
        qk = jnp.sum(qn * kn, axis=0, keepdims=True)
        for b in range(GDN_BT):
            s = s_ref[b, h]
            u = u_row[b:b + 1, h * GDN_V_DIM:(h + 1) * GDN_V_DIM]
            v_new = u - jnp.sum(wn[:, b:b + 1] * s, axis=0, keepdims=True)
            o = jnp.sum(qd[:, b:b + 1] * s, axis=0, keepdims=True) + qk[:, b:b + 1] * v_new
            sn_ref[b, h] = s * eg[:, b:b + 1] + kn[:, b:b + 1] * v_new
            o_scr[b:b + 1, h * GDN_V_DIM:(h + 1) * GDN_V_DIM] = o
    o = o_scr[...]
    r = _iota((GDN_V_W, GDN_V_W), 0) // GDN_V_DIM
    c = _iota((GDN_V_W, GDN_V_W), 1) // GDN_V_DIM
    ms = _dot_xc(o * o, jnp.where(r == c, 1.0, 0.0).astype(BF16)) * (1.0 / GDN_V_DIM)
    zf = z_ref[...].astype(F32)
    o_ref[...] = (o * lax.rsqrt(ms + RMS_EPS) * nw_ref[...] * (zf * _sigmoid(zf))).astype(BF16)


def _gdn_sample(gqkv, conv0, ba, z, state, conv_w, alog, dtb, norm_w):
    db = gqkv.shape[0]
    nt = db // GDN_BT
    qk_w = 2 * GDN_QK_W
    gt = gqkv[:, :qk_w].T.reshape(qk_w, nt, GDN_BT).transpose(1, 0, 2)
    c0t = conv0[:, :, :qk_w].transpose(1, 2, 0).reshape(3, qk_w, nt, GDN_BT).transpose(2, 0, 1, 3)
    bat = ba[:, :2 * GDN_HEADS].T.reshape(2 * GDN_HEADS, nt, GDN_BT).transpose(1, 0, 2)
    cwt = conv_w[:, :qk_w].reshape(GDN_CONV, qk_w, 1)
    alog_c = alog.astype(F32).reshape(GDN_HEADS, 1)
    dtb_c = dtb.astype(F32).reshape(GDN_HEADS, 1)
    nw = jnp.tile(norm_w.astype(F32), GDN_HEADS).reshape(1, GDN_V_W)
    full = lambda shape: pl.BlockSpec(shape, lambda i: (0,) * len(shape))
    return pl.pallas_call(
        _gdn_sample_kernel,
        grid=(nt,),
        in_specs=[pl.BlockSpec((GDN_BT, GDN_CONV_W), lambda i: (i, 0)),
                  pl.BlockSpec((GDN_BT, GDN_CONV - 1, GDN_CONV_W), lambda i: (i, 0, 0)),
                  pl.BlockSpec((1, qk_w, GDN_BT), lambda i: (i, 0, 0)),
                  pl.BlockSpec((1, GDN_CONV - 1, qk_w, GDN_BT), lambda i: (i, 0, 0, 0)),
                  pl.BlockSpec((GDN_BT, LANES), lambda i: (i, 0)),
                  pl.BlockSpec((1, 2 * GDN_HEADS, GDN_BT), lambda i: (i, 0, 0)),
                  pl.BlockSpec((GDN_BT, GDN_V_W), lambda i: (i, 0)),
                  pl.BlockSpec((GDN_BT, GDN_HEADS, GDN_K_DIM, GDN_V_DIM), lambda i: (i, 0, 0, 0)),
                  full((GDN_CONV, GDN_CONV_W)), full((GDN_CONV, qk_w, 1)),
                  full((GDN_HEADS, 1)), full((GDN_HEADS, 1)), full((1, GDN_V_W))],
        out_specs=[pl.BlockSpec((GDN_BT, GDN_V_W), lambda i: (i, 0)),
                   pl.BlockSpec((GDN_BT, GDN_HEADS, GDN_K_DIM, GDN_V_DIM), lambda i: (i, 0, 0, 0))],
        out_shape=[jax.ShapeDtypeStruct((db, GDN_V_W), BF16),
                   jax.ShapeDtypeStruct(state.shape, F32)],
        scratch_shapes=[pltpu.VMEM((GDN_BT, GDN_V_W), F32)],
        compiler_params=_cparams("parallel"),
        name="gdn_sample",
    )(gqkv, conv0, gt, c0t, ba, bat, z, state, conv_w, cwt, alog_c, dtb_c, nw)


def _mem_prompt_kernel(q_ref, k_ref, v_ref, o_ref):
    scale = MEM_HEAD_DIM ** -0.5
    for h in range(MEM_HEADS):
        lanes = slice(h * MEM_HEAD_DIM, (h + 1) * MEM_HEAD_DIM)
        s = _dot_nt(q_ref[0, :, lanes], k_ref[0, :, lanes].astype(BF16)) * scale
        m = jnp.max(s, axis=-1, keepdims=True)
        e = jnp.exp(s - m)
        p = e / jnp.sum(e, axis=-1, keepdims=True)
        o_ref[0, :, lanes] = _dot(p.astype(BF16), v_ref[0, :, lanes].astype(BF16)).astype(BF16)


def _mem_prompt(q, mem_k, mem_v, tq):
    b, seq, _ = q.shape
    return pl.pallas_call(
        _mem_prompt_kernel,
        grid=(b, seq // tq),
        in_specs=[pl.BlockSpec((1, tq, MEM_W), lambda i, j: (i, j, 0)),
                  pl.BlockSpec((1, N_MEM, MEM_W), lambda i, j: (i, 0, 0)),
                  pl.BlockSpec((1, N_MEM, MEM_W), lambda i, j: (i, 0, 0))],
        out_specs=pl.BlockSpec((1, tq, MEM_W), lambda i, j: (i, j, 0)),
        out_shape=jax.ShapeDtypeStruct((b, seq, MEM_W), BF16),
        compiler_params=_cparams("parallel", "parallel"),
        name="mem_prompt",
    )(q, mem_k, mem_v)


MEM_BT = 4


def _mem_sample_kernel(q_ref, k_ref, v_ref, o_ref):
    scale = MEM_HEAD_DIM ** -0.5
    rows = 8
    head_of_lane = _iota((rows, MEM_W), 1) // MEM_HEAD_DIM
    own = head_of_lane == _iota((rows, MEM_W), 0)
    heads = range(MEM_HEADS)
    for b in range(MEM_BT):
        qm = jnp.where(own, q_ref[b].astype(F32), 0.0).astype(BF16)
        kb = jnp.concatenate([k_ref[b, :, h, :] for h in heads], axis=1).astype(BF16)
        vb = jnp.concatenate([v_ref[b, :, h, :] for h in heads], axis=1).astype(BF16)
        s = _dot_nt(qm, kb) * scale
        m = jnp.max(s, axis=-1, keepdims=True)
        e = jnp.exp(s - m)
        p = e / jnp.sum(e, axis=-1, keepdims=True)
        o = _dot(p.astype(BF16), vb)
        o_ref[b] = jnp.sum(jnp.where(own, o, 0.0), axis=0, keepdims=True).astype(BF16)


def _mem_sample(q, mem_k, mem_v):
    db = q.shape[0]
    out = pl.pallas_call(
        _mem_sample_kernel,
        grid=(db // MEM_BT,),
        in_specs=[pl.BlockSpec((MEM_BT, 1, MEM_W), lambda i: (i, 0, 0)),
                  pl.BlockSpec((MEM_BT, N_MEM, MEM_HEADS, MEM_HEAD_DIM), lambda i: (i, 0, 0, 0)),
                  pl.BlockSpec((MEM_BT, N_MEM, MEM_HEADS, MEM_HEAD_DIM), lambda i: (i, 0, 0, 0))],
        out_specs=pl.BlockSpec((MEM_BT, 1, MEM_W), lambda i: (i, 0, 0)),
        out_shape=jax.ShapeDtypeStruct((db, 1, MEM_W), BF16),
        compiler_params=_cparams("parallel"),
        name="mem_sample",
    )(q.reshape(db, 1, MEM_W), mem_k, mem_v)
    return out.reshape(db, MEM_W)


def _merge_kernel(x_ref, osb_ref, ogdn_ref, omem_ref, wg_ref, wsb_ref, wgdn_ref, wmem_ref, wo_ref, g_ref, b_ref,
                  h_ref):
    x = x_ref[...]
    xb = x.astype(BF16)
    merged = None
    for k, (o_ref, w_ref) in enumerate(((osb_ref, wsb_ref), (ogdn_ref, wgdn_ref), (omem_ref, wmem_ref))):
        gate = _sigmoid(_dot(xb, wg_ref[:, k * D_MODEL:(k + 1) * D_MODEL]))
        term = gate * _dot(o_ref[...], w_ref[...])
        merged = term if merged is None else merged + term
    pre = DN_ALPHA * x + _dot(merged.astype(BF16), wo_ref[...])
    h_ref[...] = _layer_norm(pre, g_ref[...], b_ref[...])


def _merge_ln(x2d, o_sb, o_gdn, o_mem, w_gates, w_sb, w_gdn, w_mem, w_o, g, b, tm):
    t = x2d.shape[0]
    row = lambda w: pl.BlockSpec((tm, w), lambda i: (i, 0))
    return pl.pallas_call(
        _merge_kernel,
        grid=(t // tm,),
        in_specs=[row(D_MODEL), row(512), row(512), row(512),
                  _const_spec((D_MODEL, 3 * D_MODEL)), _const_spec((512, D_MODEL)), _const_spec((512, D_MODEL)),
                  _const_spec((512, D_MODEL)), _const_spec((D_MODEL, D_MODEL)),
                  _const_spec((1, D_MODEL)), _const_spec((1, D_MODEL))],
        out_specs=row(D_MODEL),
        out_shape=jax.ShapeDtypeStruct((t, D_MODEL), F32),
        compiler_params=_cparams("parallel"),
        name="merge_ln",
    )(x2d, o_sb, o_gdn, o_mem, w_gates, w_sb, w_gdn, w_mem, w_o, g, b)


ROUTE_TM = 128
GROUP_SIZE = N_EXPERTS // N_GROUPS


def _first_max(x, idx, n):
    m = jnp.max(x, axis=0, keepdims=True)
    first = jnp.min(jnp.where(x == m, idx, n), axis=0, keepdims=True)
    return m, first


U32 = jnp.uint32
HALF = D_MODEL // 2


def _pack_rows(x):
    bits = pltpu.bitcast(x.astype(BF16).astype(F32), U32)
    return bits[:, :HALF] | (bits[:, HALF:] >> 16)


def _unpack_rows(w):
    return pltpu.bitcast(w & U32(0xFFFF0000), F32), pltpu.bitcast(w << 16, F32)


def _two_group_specs(tm, n_first, width):
    return [pl.BlockSpec((tm, width), lambda i: (jnp.minimum(i, n_first - 1), 0)),
            pl.BlockSpec((tm, width), lambda i: (jnp.maximum(i - n_first, 0), 0))]


def _route_kernel(n_first, hp_ref, hs_ref, wr_ref, bias_ref, e_ref, rank_ref, w_ref, cnt_ref, pk_ref, carry):
    i = pl.program_id(0)

    @pl.when(i == 0)
    def _():
        carry[...] = jnp.zeros_like(carry)

    tm = hp_ref.shape[0]
    neg = -jnp.inf
    h = jnp.where(i < n_first, hp_ref[...], hs_ref[...])
    pk_ref[...] = _pack_rows(h)
    hh, hl = _split2(h)
    wh, wl = _split2(wr_ref[...])
    logits = _dot_nt(wh, hh) + _dot_nt(wh, hl) + _dot_nt(wl, hh)
    scores = _sigmoid(logits)
    biased = scores + bias_ref[...]
    grp = []
    sub = _iota((GROUP_SIZE, tm), 0)
    for g in range(N_GROUPS):
        xg = biased[g * GROUP_SIZE:(g + 1) * GROUP_SIZE]
        m1, f1 = _first_max(xg, sub, GROUP_SIZE)
        m2 = jnp.max(jnp.where(sub == f1, neg, xg), axis=0, keepdims=True)
        grp.append(m1 + m2)
    gs = jnp.concatenate(grp, axis=0)
    gidx = _iota((N_GROUPS, tm), 0)
    gsel = gidx < 0
    for _ in range(TOPK_GROUPS):
        _, f = _first_max(gs, gidx, N_GROUPS)
        hit = gidx == f
        gsel = jnp.logical_or(gsel, hit)
        gs = jnp.where(hit, neg, gs)
    eidx = _iota((N_EXPERTS, tm), 0)
    gsel_f = jnp.where(gsel, 1.0, 0.0)
    allowed = jnp.concatenate(
        [jnp.broadcast_to(gsel_f[g:g + 1], (GROUP_SIZE, tm)) for g in range(N_GROUPS)], axis=0) > 0.5
    choice = jnp.where(allowed, biased, neg)
    picks = []
    weights = []
    chosen = jnp.zeros((N_EXPERTS, tm), F32)
    for _ in range(TOP_K):
        _, f = _first_max(choice, eidx, N_EXPERTS)
        hit = eidx == f
        picks.append(f)
        weights.append(jnp.sum(jnp.where(hit, scores, 0.0), axis=0, keepdims=True))
        chosen = jnp.where(hit, 1.0, chosen)
        choice = jnp.where(hit, neg, choice)
    wsum = weights[0]
    for w in weights[1:]:
        wsum = wsum + w
    tr = _iota((tm, tm), 0)
    tc = _iota((tm, tm), 1)
    before = jnp.where(tr < tc, 1.0, 0.0).astype(BF16)
    chosen_b = chosen.astype(BF16)
    rank = _dot(chosen_b, before) + carry[...]
    ranks = [jnp.sum(jnp.where(eidx == f, rank, 0.0), axis=0, keepdims=True) for f in picks]
    e_ref[...] = jnp.concatenate(picks, axis=0)
    rank_ref[...] = jnp.concatenate(ranks, axis=0).astype(I32)
    w_ref[...] = jnp.concatenate(weights, axis=0) / wsum * ROUTE_SCALE
    carry[...] = carry[...] + _dot(chosen_b, jnp.ones((tm, LANES), BF16))
    cnt_ref[...] = carry[...]


def _route(h_first, h_second, w_router_t, bias_col):
    tm = ROUTE_TM
    n_first = h_first.shape[0] // tm
    t = h_first.shape[0] + h_second.shape[0]
    col = lambda: pl.BlockSpec((TOP_K, tm), lambda i: (0, i))
    return pl.pallas_call(
        functools.partial(_route_kernel, n_first),
        grid=(t // tm,),
        in_specs=_two_group_specs(tm, n_first, D_MODEL)
        + [pl.BlockSpec((N_EXPERTS, D_MODEL), lambda i: (0, 0)),
           pl.BlockSpec((N_EXPERTS, 1), lambda i: (0, 0))],
        out_specs=[col(), col(), col(), pl.BlockSpec((N_EXPERTS, LANES), lambda i: (0, 0)),
                   pl.BlockSpec((tm, HALF), lambda i: (i, 0))],
        out_shape=[jax.ShapeDtypeStruct((TOP_K, t), I32), jax.ShapeDtypeStruct((TOP_K, t), I32),
                   jax.ShapeDtypeStruct((TOP_K, t), F32), jax.ShapeDtypeStruct((N_EXPERTS, LANES), F32),
                   jax.ShapeDtypeStruct((t, HALF), U32)],
        scratch_shapes=[pltpu.VMEM((N_EXPERTS, LANES), F32)],
        compiler_params=_cparams("arbitrary"),
        name="route",
    )(h_first, h_second, w_router_t, bias_col)


DISPATCH_TM = 128


def _row_copy(src, dst, s_row, d_row, sem):
    return pltpu.make_async_copy(src.at[pl.ds(s_row, 1)], dst.at[pl.ds(d_row, 1)], sem)


def _dispatch_kernel(dest_ref, h_ref, xs_ref, sem):
    def issue(t, _):
        for r in range(TOP_K):
            _row_copy(h_ref, xs_ref, t, dest_ref[t * TOP_K + r], sem).start()
        return 0

    lax.fori_loop(0, DISPATCH_TM, issue, 0)

    def drain(t, _):
        for r in range(TOP_K):
            _row_copy(h_ref, xs_ref, 0, 0, sem).wait()
        return 0

    lax.fori_loop(0, DISPATCH_TM, drain, 0)


def _dispatch(h_packed, dest_flat):
    t = h_packed.shape[0]
    return pl.pallas_call(
        _dispatch_kernel,
        grid=(t // DISPATCH_TM,),
        in_specs=[pl.BlockSpec((DISPATCH_TM * TOP_K,), lambda i: (i,), memory_space=pltpu.SMEM),
                  pl.BlockSpec((DISPATCH_TM, HALF), lambda i: (i, 0))],
        out_specs=pl.BlockSpec(memory_space=pl.ANY),
        out_shape=jax.ShapeDtypeStruct((t * TOP_K, HALF), U32),
        scratch_shapes=[pltpu.SemaphoreType.DMA(())],
        compiler_params=_cparams("arbitrary"),
        name="moe_dispatch",
    )(dest_flat, h_packed)


def _expert_kernel(blk_ref, exp_ref, lo_ref, hi_ref, first_ref, x_ref, wg_ref, wu_ref, wd_ref, y_ref, wgu, wdn):
    del blk_ref
    i = pl.program_id(0)

    @pl.when((i == 0) | (exp_ref[i] != exp_ref[jnp.maximum(i - 1, 0)]))
    def _():
        wgu[:, 0:D_EXPERT] = wg_ref[0].astype(BF16)
        wgu[:, D_EXPERT:] = wu_ref[0].astype(BF16)
        wdn[...] = wd_ref[0].astype(BF16)

    x = jnp.concatenate(_unpack_rows(x_ref[...]), axis=1).astype(BF16)
    gu = _dot(x, wgu[...])
    gate = gu[:, 0:D_EXPERT]
    act = gate * _sigmoid(gate) * gu[:, D_EXPERT:]
    y = _pack_rows(_dot(act.astype(BF16), wdn[...]))
    row = _iota((ROW_BLOCK, 1), 0)
    mine = (row >= lo_ref[i]) & (row < hi_ref[i])

    @pl.when(first_ref[i] == 1)
    def _():
        y_ref[...] = jnp.where(mine, y, U32(0))

    @pl.when(first_ref[i] == 0)
    def _():
        y_ref[...] = jnp.where(mine, y, y_ref[...])


def _expert_mm(xs, w_gate, w_up, w_down, blk, exp, lo, hi, first):
    n_items = blk.shape[0]
    grid_spec = pltpu.PrefetchScalarGridSpec(
        num_scalar_prefetch=5,
        grid=(n_items,),
        in_specs=[pl.BlockSpec((ROW_BLOCK, HALF), lambda i, blk, exp, lo, hi, fi: (blk[i], 0)),
                  pl.BlockSpec((1, D_MODEL, D_EXPERT), lambda i, blk, exp, lo, hi, fi: (exp[i], 0, 0)),
                  pl.BlockSpec((1, D_MODEL, D_EXPERT), lambda i, blk, exp, lo, hi, fi: (exp[i], 0, 0)),
                  pl.BlockSpec((1, D_EXPERT, D_MODEL), lambda i, blk, exp, lo, hi, fi: (exp[i], 0, 0))],
        out_specs=pl.BlockSpec((ROW_BLOCK, HALF), lambda i, blk, exp, lo, hi, fi: (blk[i], 0)),
        scratch_shapes=[pltpu.VMEM((D_MODEL, 2 * D_EXPERT), BF16), pltpu.VMEM((D_EXPERT, D_MODEL), BF16)],
    )
    return pl.pallas_call(
        _expert_kernel,
        grid_spec=grid_spec,
        out_shape=jax.ShapeDtypeStruct(xs.shape, U32),
        compiler_params=_cparams("arbitrary"),
        name="moe_experts",
    )(blk, exp, lo, hi, first, xs, w_gate, w_up, w_down)


COMBINE_TM = 128


def _combine_kernel(n_first, dest_ref, hp_ref, hs_ref, wt_ref, y_ref, wsg_ref, wsd_ref, g_ref, b_ref,
                    op_ref, os_ref, buf, sem):
    i = pl.program_id(0)

    def issue(t, _):
        for r in range(TOP_K):
            pltpu.make_async_copy(y_ref.at[pl.ds(dest_ref[t * TOP_K + r], 1)], buf.at[r, pl.ds(t, 1)], sem).start()
        return 0

    lax.fori_loop(0, COMBINE_TM, issue, 0)
    h = jnp.where(i < n_first, hp_ref[...], hs_ref[...])
    gu = _dot(h.astype(BF16), wsg_ref[...])
    gate = gu[:, 0:D_SHARED]
    f = _dot((gate * _sigmoid(gate) * gu[:, D_SHARED:]).astype(BF16), wsd_ref[...])

    def drain(t, _):
        for r in range(TOP_K):
            pltpu.make_async_copy(y_ref.at[pl.ds(0, 1)], buf.at[r, pl.ds(t, 1)], sem).wait()
        return 0

    lax.fori_loop(0, COMBINE_TM, drain, 0)
    lo = jnp.zeros((COMBINE_TM, HALF), F32)
    hi = jnp.zeros((COMBINE_TM, HALF), F32)
    for r in range(TOP_K):
        ya, yb = _unpack_rows(buf[r])
        w = wt_ref[:, r:r + 1]
        lo = lo + w * ya
        hi = hi + w * yb
    out = _layer_norm(DN_ALPHA * h + f + jnp.concatenate([lo, hi], axis=1), g_ref[...], b_ref[...])

    @pl.when(i < n_first)
    def _():
        op_ref[...] = out

    @pl.when(i >= n_first)
    def _():
        os_ref[...] = out


def _combine(h_first, h_second, w_tok, dest_flat, y, w_sgu, w_sd, g, b):
    tm = COMBINE_TM
    n_first = h_first.shape[0] // tm
    t = h_first.shape[0] + h_second.shape[0]
    return pl.pallas_call(
        functools.partial(_combine_kernel, n_first),
        grid=(t // tm,),
        in_specs=[pl.BlockSpec((tm * TOP_K,), lambda i: (i,), memory_space=pltpu.SMEM)]
        + _two_group_specs(tm, n_first, D_MODEL)
        + [pl.BlockSpec((tm, TOP_K), lambda i: (i, 0)),
           pl.BlockSpec(memory_space=pl.ANY),
           _const_spec((D_MODEL, 2 * D_SHARED)), _const_spec((D_SHARED, D_MODEL)),
           _const_spec((1, D_MODEL)), _const_spec((1, D_MODEL))],
        out_specs=_two_group_specs(tm, n_first, D_MODEL),
        out_shape=[jax.ShapeDtypeStruct(h_first.shape, F32), jax.ShapeDtypeStruct(h_second.shape, F32)],
        scratch_shapes=[pltpu.VMEM((TOP_K, tm, HALF), U32), pltpu.SemaphoreType.DMA(())],
        compiler_params=_cparams("arbitrary"),
        name="moe_combine",
    )(dest_flat, h_first, h_second, w_tok, y, w_sgu, w_sd, g, b)


def _moe_plan(counts, n_rows):
    n_blocks = n_rows // ROW_BLOCK
    n_items = n_blocks + N_EXPERTS
    start = jnp.cumsum(counts) - counts
    end = start + counts
    first_blk = start // ROW_BLOCK
    last_blk = jnp.maximum(end - 1, start) // ROW_BLOCK
    nb = jnp.where(counts > 0, last_blk - first_blk + 1, 0)
    item_end = jnp.cumsum(nb)
    item_start = item_end - nb
    total = item_end[-1]
    idx = jnp.arange(n_items, dtype=I32)
    last_item = jnp.maximum(total - 1, 0)
    src = jnp.minimum(idx, last_item)
    exp = jnp.minimum(jnp.sum(item_end[None, :] <= src[:, None], axis=1), N_EXPERTS - 1).astype(I32)
    onehot = exp[:, None] == jnp.arange(N_EXPERTS, dtype=I32)[None, :]
    take = lambda table: jnp.sum(jnp.where(onehot, table[None, :], 0), axis=1)
    blk = (take(first_blk) + src - take(item_start)).astype(I32)
    valid = idx < total
    lo = jnp.clip(take(start) - blk * ROW_BLOCK, 0, ROW_BLOCK)
    hi = jnp.clip(take(end) - blk * ROW_BLOCK, 0, ROW_BLOCK)
    first = (valid & (lo == 0)).astype(I32)
    lo = jnp.where(valid, lo, 0).astype(I32)
    hi = jnp.where(valid, hi, 0).astype(I32)
    return start, blk, exp, lo, hi, first


def _moe(h_first, h_second, w_router, router_bias, w_exp_gate, w_exp_up, w_exp_down, w_sgu, w_sd, g, b):
    t = h_first.shape[0] + h_second.shape[0]
    picks, ranks, weights, cnt, h_packed = _route(h_first, h_second, w_router.T,
                                                  router_bias.astype(F32).reshape(N_EXPERTS, 1))
    counts = cnt[:, 0].astype(I32)
    start, blk, exp, lo, hi, first = _moe_plan(counts, t * TOP_K)
    experts = jnp.arange(N_EXPERTS, dtype=I32)[:, None, None]
    dest = ranks + jnp.sum(jnp.where(picks[None] == experts, start.astype(I32)[:, None, None], 0), axis=0)
    dest_flat = dest.T.reshape(t * TOP_K)
    xs = _dispatch(h_packed, dest_flat)
    y = _expert_mm(xs, w_exp_gate, w_exp_up, w_exp_down, blk, exp, lo, hi, first)
    return _combine(h_first, h_second, weights.T, dest_flat, y, w_sgu, w_sd, g, b)


def kernel(x_prompt, x_sample, cache_sb_k, cache_sb_v, page_table, state_gdn, state_gdn_conv, cache_mem_k, cache_mem_v, mem_prompt, w_in, sb_logit_bias, conv_w, gdn_A_log, gdn_dt_bias, gdn_norm_w, w_mem_k, w_mem_v, w_br_sb, w_br_gdn, w_br_mem, w_o, ln1_g, ln1_b, w_router, router_bias, w_exp_gate, w_exp_up, w_exp_down, w_sh_gate, w_sh_up, w_sh_down, ln2_g, ln2_b):
    bsz, seq, _ = x_prompt.shape
    db = x_sample.shape[0]
    tp = bsz * seq

    ba_off = 3 * SB_W + GDN_CONV_W + GDN_V_W
    mq_off = ba_off + 2 * GDN_HEADS
    gate_off = mq_off + MEM_W
    w_main = jnp.concatenate([w_in[:, :SB_W], w_in[:, 3 * SB_W:ba_off], w_in[:, mq_off:gate_off]],
                             axis=1).astype(BF16)
    w_kvt = w_in[:, SB_W:3 * SB_W].T.astype(BF16)
    w_ba = jnp.pad(w_in[:, ba_off:mq_off], ((0, 0), (0, LANES - 2 * GDN_HEADS))).astype(BF16)
    w_gates = w_in[:, gate_off:].astype(BF16)
    w_kv = jnp.concatenate([w_mem_k, w_mem_v], axis=1).astype(BF16)
    w_sb, w_gdn, w_mem, w_ob = (w.astype(BF16) for w in (w_br_sb, w_br_gdn, w_br_mem, w_o))
    w_sgu = jnp.concatenate([w_sh_gate, w_sh_up], axis=1).astype(BF16)
    w_sd = w_sh_down.astype(BF16)
    row = lambda v: v.astype(F32).reshape(1, -1)
    head_lane = lambda v: jnp.pad(v.astype(F32), (GDN_HEADS, LANES - 2 * GDN_HEADS)).reshape(1, LANES)
    nw_pair = jnp.tile(gdn_norm_w.astype(F32), 2).reshape(1, LANES)
    bias = sb_logit_bias.astype(F32)

    xp = x_prompt.reshape(tp, D_MODEL)
    q, kt, vt, gqkv, z, mq, ba = _proj_in(xp, seq, w_main, w_kvt, w_ba, min(512, seq))
    mem_k, mem_v = _mem_kv(mem_prompt.reshape(bsz * N_MEM, D_MODEL), w_kv, 512)
    sh = lambda a: a.reshape(bsz, seq, a.shape[-1])
    o_sb = _sb_prompt(sh(q), kt, vt, bias)
    gq3 = sh(gqkv)
    o_gdn, s_p = _gdn_prompt(gq3, sh(z), sh(ba), conv_w.astype(F32), head_lane(gdn_A_log),
                             head_lane(gdn_dt_bias), nw_pair)
    o_mem = _mem_prompt(sh(mq), mem_k.reshape(bsz, N_MEM, MEM_W), mem_v.reshape(bsz, N_MEM, MEM_W), min(512, seq))
    h_p = _merge_ln(xp, o_sb.reshape(tp, SB_W), o_gdn.reshape(tp, GDN_V_W), o_mem.reshape(tp, MEM_W),
                    w_gates, w_sb, w_gdn, w_mem, w_ob, row(ln1_g), row(ln1_b), 512)

    xs = x_sample.reshape(db, D_MODEL)
    q_s, kt_s, vt_s, gqkv_s, z_s, mq_s, ba_s = _proj_in(xs, db, w_main, w_kvt, w_ba, db)
    o_sb_s = _sb_sample(q_s, cache_sb_k, cache_sb_v, page_table, bias)
    o_gdn_s, s_s = _gdn_sample(gqkv_s, state_gdn_conv.astype(F32), ba_s, z_s, state_gdn.astype(F32),
                               conv_w.astype(F32), gdn_A_log, gdn_dt_bias, gdn_norm_w)
    o_mem_s = _mem_sample(mq_s, cache_mem_k, cache_mem_v)
    h_s = _merge_ln(xs, o_sb_s, o_gdn_s, o_mem_s, w_gates, w_sb, w_gdn, w_mem, w_ob, row(ln1_g), row(ln1_b), db)

    y_p, y_s = _moe(h_p, h_s, w_router, router_bias, w_exp_gate, w_exp_up, w_exp_down,
                    w_sgu, w_sd, row(ln2_g), row(ln2_b))

    heads_p = lambda a: a.reshape(bsz, SB_HEADS, SB_HEAD_DIM, seq).transpose(0, 3, 1, 2)
    heads_s = lambda a: a[0].T.reshape(db, 1, SB_HEADS, SB_HEAD_DIM)
    return (y_p.reshape(bsz, seq, D_MODEL), y_s.reshape(db, 1, D_MODEL),
            heads_p(kt), heads_p(vt),
            s_p, gq3[:, seq - (GDN_CONV - 1):, :],
            mem_k.reshape(bsz, N_MEM, MEM_HEADS, MEM_HEAD_DIM), mem_v.reshape(bsz, N_MEM, MEM_HEADS, MEM_HEAD_DIM),
            heads_s(kt_s), heads_s(vt_s),
            s_s, jnp.concatenate([state_gdn_conv[:, 1:, :], gqkv_s[:, None, :]], axis=1))
```

```python
import functools

import jax
import jax.numpy as jnp
from jax import lax
from jax.experimental import pallas as pl
from jax.experimental.pallas import tpu as pltpu

F32 = jnp.float32
BF16 = jnp.bfloat16
I32 = jnp.int32

D_MODEL = 1024
PAGE_SIZE = 128
SB_HEADS = 8
SB_HEAD_DIM = 64
SB_W = SB_HEADS * SB_HEAD_DIM
GDN_HEADS = 8
GDN_K_DIM = 64
GDN_V_DIM = 64
GDN_CONV = 4
GDN_QK_W = GDN_HEADS * GDN_K_DIM
GDN_V_W = GDN_HEADS * GDN_V_DIM
GDN_CONV_W = 2 * GDN_QK_W + GDN_V_W
N_MEM = 256
MEM_HEADS = 4
MEM_HEAD_DIM = 128
MEM_W = MEM_HEADS * MEM_HEAD_DIM
N_EXPERTS = 64
TOP_K = 8
N_GROUPS = 8
TOPK_GROUPS = 4
D_EXPERT = 256
D_SHARED = 256
ROUTE_SCALE = 2.5
DEPTH = 1
DN_ALPHA = (2 * DEPTH) ** 0.25
LN_EPS = 1e-5
RMS_EPS = 1e-6
L2_EPS = 1e-6

LANES = 128
CHUNK = 128
ROW_BLOCK = 512
VMEM_LIMIT = 56 * 1024 * 1024


def _cparams(*sem):
    return pltpu.CompilerParams(dimension_semantics=sem, vmem_limit_bytes=VMEM_LIMIT)


def _dot(a, b):
    return jnp.dot(a, b, preferred_element_type=F32)


def _dot_nt(a, b):
    return lax.dot_general(a, b, (((1,), (1,)), ((), ())), preferred_element_type=F32)


def _split2(x):
    hi = x.astype(BF16)
    lo = (x - hi.astype(F32)).astype(BF16)
    return hi, lo


def _split3(x):
    hi = x.astype(BF16)
    r = x - hi.astype(F32)
    mid = r.astype(BF16)
    lo = (r - mid.astype(F32)).astype(BF16)
    return hi, mid, lo


def _dot_xc(x, m):
    return _dot(jnp.concatenate(_split3(x), axis=1), jnp.concatenate([m, m, m], axis=0))


def _dot_cx(m, x):
    return _dot(jnp.concatenate([m, m, m], axis=1), jnp.concatenate(_split3(x), axis=0))


def _dot_ff(a, b):
    ah, al = _split2(a)
    bh, bl = _split2(b)
    return _dot(jnp.concatenate([ah, ah, al], axis=1), jnp.concatenate([bh, bl, bh], axis=0))


def _sigmoid(x):
    return 1.0 / (1.0 + jnp.exp(-x))


def _softplus(x):
    return jnp.maximum(x, 0.0) + jnp.log(1.0 + jnp.exp(-jnp.abs(x)))


def _iota(shape, dim):
    return lax.broadcasted_iota(I32, shape, dim)


def _const_spec(shape):
    nd = len(shape)
    return pl.BlockSpec(shape, lambda *_: (0,) * nd, pipeline_mode=pl.Buffered(1))


def _layer_norm(x, g, b):
    mu = jnp.mean(x, axis=-1, keepdims=True)
    xc = x - mu
    var = jnp.mean(xc * xc, axis=-1, keepdims=True)
    return xc * lax.rsqrt(var + LN_EPS) * g + b


def _proj_kernel(x_ref, w_ref, wkv_ref, wba_ref, q_ref, kt_ref, vt_ref, g_ref, z_ref, mq_ref, ba_ref):
    xb = x_ref[...].astype(BF16)
    q_ref[...] = (_dot(xb, w_ref[:, 0:512]) * 0.125).astype(BF16)
    kvt = _dot_nt(wkv_ref[...], xb)
    kt_ref[0] = kvt[0:SB_W]
    vt_ref[0] = kvt[SB_W:]
    for c in range(3):
        g_ref[:, c * 512:(c + 1) * 512] = _dot(xb, w_ref[:, 512 + c * 512:1024 + c * 512])
    z_ref[...] = _dot(xb, w_ref[:, 2048:2560]).astype(BF16)
    mq_ref[...] = _dot(xb, w_ref[:, 2560:3072]).astype(BF16)
    ba_ref[...] = _dot(xb, wba_ref[...])


def _proj_in(x2d, seq, w_main, w_kvt, w_ba, tm):
    t = x2d.shape[0]
    per_seq = seq // tm
    row = lambda w: pl.BlockSpec((tm, w), lambda i: (i, 0))
    tr = lambda: pl.BlockSpec((1, SB_W, tm), lambda i: (i // per_seq, 0, i % per_seq))
    return pl.pallas_call(
        _proj_kernel,
        grid=(t // tm,),
        in_specs=[row(D_MODEL), _const_spec((D_MODEL, 3072)), _const_spec((2 * SB_W, D_MODEL)),
                  _const_spec((D_MODEL, LANES))],
        out_specs=[row(512), tr(), tr(), row(1536), row(512), row(512), row(LANES)],
        out_shape=[jax.ShapeDtypeStruct((t, 512), BF16), jax.ShapeDtypeStruct((t // seq, SB_W, seq), F32),
                   jax.ShapeDtypeStruct((t // seq, SB_W, seq), F32), jax.ShapeDtypeStruct((t, 1536), F32),
                   jax.ShapeDtypeStruct((t, 512), BF16), jax.ShapeDtypeStruct((t, 512), BF16),
                   jax.ShapeDtypeStruct((t, LANES), F32)],
        compiler_params=_cparams("parallel"),
        name="proj_in",
    )(x2d, w_main, w_kvt, w_ba)


def _memkv_kernel(x_ref, w_ref, k_ref, v_ref):
    xb = x_ref[...].astype(BF16)
    k_ref[...] = _dot(xb, w_ref[:, 0:512])
    v_ref[...] = _dot(xb, w_ref[:, 512:1024])


def _mem_kv(x2d, w_kv, tm):
    t = x2d.shape[0]
    row = lambda w: pl.BlockSpec((tm, w), lambda i: (i, 0))
    return pl.pallas_call(
        _memkv_kernel,
        grid=(t // tm,),
        in_specs=[row(D_MODEL), _const_spec((D_MODEL, 1024))],
        out_specs=[row(512), row(512)],
        out_shape=[jax.ShapeDtypeStruct((t, 512), F32)] * 2,
        compiler_params=_cparams("parallel"),
        name="mem_kv",
    )(x2d, w_kv)


def _cumsum_matrix():
    r = _iota((2 * CHUNK, 2 * CHUNK), 0) % CHUNK
    c = _iota((2 * CHUNK, 2 * CHUNK), 1)
    return jnp.where((c >= CHUNK) | (r > c), 1.0, 0.0).astype(BF16)


SB_QT = 512


SB_KB = 2


def _sb_prompt_kernel(qt, bias_ref, q_ref, k_ref, v_ref, o_ref, kms, vms, carry, acc):
    hp = pl.program_id(1)
    seq = q_ref.shape[1]
    nblk = seq // CHUNK
    span = SB_KB * CHUNK
    first = _iota((LANES, CHUNK), 0) < SB_HEAD_DIM
    tt = _cumsum_matrix()
    dist = _iota((qt, LANES), 0) - _iota((qt, LANES), 1)
    biases = (bias_ref[2 * hp], bias_ref[2 * hp + 1])

    def stage(i, _):
        cols = pl.ds(pl.multiple_of(i * CHUNK, CHUNK), CHUNK)
        base = pl.multiple_of(i * 2 * CHUNK, 2 * CHUNK)
        for src, dst in ((k_ref, kms), (v_ref, vms)):
            x = src[0, :, cols]
            dst[:, pl.ds(base, CHUNK)] = jnp.where(first, x, 0.0).astype(BF16)
            dst[:, pl.ds(base + CHUNK, CHUNK)] = jnp.where(first, 0.0, x).astype(BF16)
        zero = jnp.zeros((CHUNK, LANES), F32)
        carry[0, cols, :] = zero
        carry[1, cols, :] = zero
        acc[cols, :] = zero
        return 0

    lax.fori_loop(0, nblk, stage, 0)

    def kstep(pp, _):
        p = nblk // SB_KB - 1 - pp
        key0 = p * span
        staged = pl.ds(pl.multiple_of(p * 2 * span, 2 * span), 2 * span)
        km = kms[:, staged]
        vm = vms[:, staged]

        def qtile(t, _):
            row0 = pl.multiple_of(t * qt, qt)
            rows = pl.ds(row0, qt)
            z_all = _dot(q_ref[0, rows, :], km)
            zs, sps, parts, masks = [], [], [], []
            for g in range(2 * SB_KB):
                causal = dist > key0 + (g // 2) * CHUNK - row0
                z = z_all[:, g * CHUNK:(g + 1) * CHUNK] + biases[g % 2]
                sp = _softplus(z)
                hi, lo = _split2(jnp.where(causal, sp, 0.0))
                zs.append(z)
                sps.append(sp)
                masks.append(causal)
                parts.append(jnp.concatenate([hi, lo], axis=1))
            r_all = _dot(jnp.concatenate(parts, axis=0), tt)
            a = [None] * (2 * SB_KB)
            for h in range(2):
                c = carry[h, rows, :]
                for blk in reversed(range(SB_KB)):
                    g = 2 * blk + h
                    r = r_all[g * qt:(g + 1) * qt]
                    e = jnp.exp(zs[g] - sps[g] - r[:, :CHUNK] - c)
                    a[g] = jnp.where(masks[g], e, 0.0).astype(BF16)
                    c = c + r[:, CHUNK:]
                carry[h, rows, :] = c
            acc[rows, :] = acc[rows, :] + _dot_nt(jnp.concatenate(a, axis=1), vm)
            return 0

        lax.fori_loop(key0 // qt, seq // qt, qtile, 0)
        return 0

    lax.fori_loop(0, nblk // SB_KB, kstep, 0)

    def write_out(i, _):
        rows = pl.ds(pl.multiple_of(i * CHUNK, CHUNK), CHUNK)
        o_ref[0, rows, :] = acc[rows, :].astype(BF16)
        return 0

    lax.fori_loop(0, nblk, write_out, 0)


def _sb_prompt(q, kt, vt, bias):
    b, seq, _ = q.shape
    qt = min(SB_QT, seq)
    blk = lambda: pl.BlockSpec((1, seq, LANES), lambda i, j: (i, 0, j))
    blk_t = lambda: pl.BlockSpec((1, LANES, seq), lambda i, j: (i, j, 0))
    return pl.pallas_call(
        functools.partial(_sb_prompt_kernel, qt),
        grid=(b, SB_W // LANES),
        in_specs=[pl.BlockSpec(memory_space=pltpu.SMEM), blk(), blk_t(), blk_t()],
        out_specs=blk(),
        out_shape=jax.ShapeDtypeStruct((b, seq, SB_W), BF16),
        scratch_shapes=[pltpu.VMEM((LANES, 2 * seq), BF16), pltpu.VMEM((LANES, 2 * seq), BF16),
                        pltpu.VMEM((2, seq, LANES), F32), pltpu.VMEM((seq, LANES), F32)],
        compiler_params=_cparams("parallel", "parallel"),
        name="sb_prompt",
    )(bias, q, kt, vt)


def _sb_sample_kernel(n_pages, pt_ref, q_ref, bias_ref, *refs):
    del pt_ref
    k_refs = refs[:n_pages]
    v_refs = refs[n_pages:2 * n_pages]
    o_ref = refs[2 * n_pages]
    head_of_lane = _iota((SB_HEADS, SB_W), 1) // SB_HEAD_DIM
    own = head_of_lane == _iota((SB_HEADS, SB_W), 0)
    qm = jnp.where(own, q_ref[0].astype(F32), 0.0).astype(BF16)
    pages = range(n_pages)
    kcat = jnp.concatenate([k_refs[p][0].astype(BF16) for p in pages], axis=1)
    z = _dot(qm, kcat)
    zr = jnp.concatenate([z[:, p * PAGE_SIZE:(p + 1) * PAGE_SIZE] + bias_ref[...] for p in pages], axis=0)
    sp = _softplus(zr)
    hi, lo = _split2(sp)
    r = _dot(jnp.concatenate([hi, lo], axis=1), _cumsum_matrix())
    carries = [None] * n_pages
    c = jnp.zeros((SB_HEADS, LANES), F32)
    for p in reversed(pages):
        carries[p] = c
        c = c + r[p * SB_HEADS:(p + 1) * SB_HEADS, PAGE_SIZE:]
    a = jnp.exp(zr - sp - r[:, :PAGE_SIZE] - jnp.concatenate(carries, axis=0))
    acat = jnp.concatenate([a[p * SB_HEADS:(p + 1) * SB_HEADS] for p in pages], axis=1).astype(BF16)
    vcat = jnp.concatenate([v_refs[p][0].astype(BF16) for p in pages], axis=1)
    o = _dot_nt(acat, vcat)
    o_ref[0] = jnp.sum(jnp.where(own, o, 0.0), axis=0, keepdims=True).astype(BF16)


def _sb_sample(q, cache_k, cache_v, page_table, bias):
    db, n_pages = page_table.shape
    n_pool = cache_k.shape[0]
    ck = cache_k.transpose(0, 2, 3, 1).reshape(n_pool, SB_W, PAGE_SIZE)
    cv = cache_v.transpose(0, 2, 3, 1).reshape(n_pool, SB_W, PAGE_SIZE)
    bias2d = jnp.broadcast_to(bias.astype(F32)[:, None], (SB_HEADS, LANES))

    def page_spec(p):
        return pl.BlockSpec((1, SB_W, PAGE_SIZE), lambda i, pt: (pt[i, p], 0, 0))

    grid_spec = pltpu.PrefetchScalarGridSpec(
        num_scalar_prefetch=1,
        grid=(db,),
        in_specs=[pl.BlockSpec((1, 1, SB_W), lambda i, pt: (i, 0, 0)),
                  pl.BlockSpec((SB_HEADS, LANES), lambda i, pt: (0, 0))]
        + [page_spec(p) for p in range(n_pages)] * 2,
        out_specs=pl.BlockSpec((1, 1, SB_W), lambda i, pt: (i, 0, 0)),
    )
    out = pl.pallas_call(
        functools.partial(_sb_sample_kernel, n_pages),
        grid_spec=grid_spec,
        out_shape=jax.ShapeDtypeStruct((db, 1, SB_W), BF16),
        compiler_params=_cparams("parallel"),
        name="sb_sample",
    )(page_table, q.reshape(db, 1, SB_W), bias2d, *([ck] * n_pages), *([cv] * n_pages))
    return out.reshape(db, SB_W)


def _head_ones():
    r = _iota((LANES, LANES), 0) // GDN_K_DIM
    c = _iota((LANES, LANES), 1) // GDN_K_DIM
    return jnp.where(r == c, 1.0, 0.0).astype(BF16)


def _unit_lower_solve(a, rhs):
    p = -a
    y = rhs
    for _ in range(6):
        r = _dot_ff(p, jnp.concatenate([p, y], axis=1))
        y = y + r[:, CHUNK:]
        p = r[:, :CHUNK]
    return y + _dot_ff(p, y)


GDN_TILE = 512


def _gdn_prompt_parts(tile, hp, qp_ref, kp_ref, vp_ref, z_ref, ba_ref, cwq_ref, cwk_ref, cwv_ref, alog_ref, dtb_ref,
                      nw_ref, o_ref, s_ref, qs, ks, vs, qn, kn, vn, exs, os_):
    seq = qp_ref.shape[1]
    pad = 8
    row = _iota((CHUNK, CHUNK), 0)
    col = _iota((CHUNK, CHUNK), 1)
    lower = row >= col
    strict = row > col
    first = col < GDN_K_DIM
    same_head = (row // GDN_K_DIM) == (col // GDN_K_DIM)
    head_ones = _head_ones()
    lower_ones = jnp.where(lower, 1.0, 0.0).astype(BF16)
    er = _iota((LANES, 4 * LANES), 0)
    ec = _iota((LANES, 4 * LANES), 1)
    sec = ec // LANES
    pair_head = 2 * hp + (ec % LANES) // GDN_K_DIM
    want = jnp.where(sec == 0, pair_head,
                     jnp.where(sec == 1, GDN_HEADS + pair_head,
                               jnp.where(sec == 2, GDN_HEADS + 2 * hp, GDN_HEADS + 2 * hp + 1)))
    expand = jnp.where(er == want, 1.0, 0.0).astype(BF16)
    lane_row = _iota((1, LANES), 1)
    neg_a = -jnp.exp(alog_ref[...])
    scale = GDN_K_DIM ** -0.5

    n_sub = tile // CHUNK

    def conv(scr, cw_ref, r0):
        w = scr[pl.ds(r0, tile + pad), :]
        y = (w[5:5 + tile] * cw_ref[0:1, :] + w[6:6 + tile] * cw_ref[1:2, :]
             + w[7:7 + tile] * cw_ref[2:3, :] + w[8:8 + tile] * cw_ref[3:4, :])
        return y * _sigmoid(y)

    def prepare(tt):
        r0 = pl.multiple_of(tt * tile, tile)
        rows = pl.ds(r0, tile)
        q = conv(qs, cwq_ref, r0)
        k = conv(ks, cwk_ref, r0)
        v = conv(vs, cwv_ref, r0)
        ss = _dot_xc(jnp.concatenate([q * q, k * k], axis=0), head_ones)
        qn[rows, :] = q * lax.rsqrt(ss[:tile] + L2_EPS) * scale
        kn[rows, :] = k * lax.rsqrt(ss[tile:] + L2_EPS)
        vn[rows, :] = v
        raw = ba_ref[0, rows, :]
        beta_raw = _sigmoid(raw)
        g_raw = neg_a * _softplus(raw + dtb_ref[...])
        g_wide = jnp.concatenate([g_raw[i * CHUNK:(i + 1) * CHUNK] for i in range(n_sub)], axis=1)
        c_wide = _dot_cx(lower_ones, g_wide)
        gcum = jnp.concatenate([c_wide[:, i * LANES:(i + 1) * LANES] for i in range(n_sub)], axis=0)
        exs[rows, :] = _dot_xc(jnp.where(lane_row < GDN_HEADS, beta_raw, gcum), expand)

    def setup():
        for scr, src in ((qs, qp_ref), (ks, kp_ref), (vs, vp_ref)):
            scr[0:pad, :] = jnp.zeros((pad, LANES), F32)
            scr[pad:seq + pad, :] = src[0]

        def body(tt, carry):
            prepare(tt)
            return carry

        lax.fori_loop(0, seq // tile, body, 0)

    def within_chunk(c):
        rows = pl.ds(pl.multiple_of(c * CHUNK, CHUNK), CHUNK)
        q = qn[rows, :]
        k = kn[rows, :]
        v = vn[rows, :]
        ex = exs[rows, :]
        beta_p = ex[:, 0:LANES]
        gc_p = ex[:, LANES:2 * LANES]
        eg_p = jnp.exp(gc_p)
        g_last = gc_p[CHUNK - 1:CHUNK, :]
        kb = k * beta_p
        lhs = jnp.concatenate([jnp.where(first, kb, 0.0), jnp.where(first, q, 0.0),
                               jnp.where(first, 0.0, kb), jnp.where(first, 0.0, q)], axis=0).astype(BF16)
        kq = _dot_nt(lhs, k.astype(BF16))
        rhs = jnp.concatenate([v * beta_p, kb * eg_p], axis=1)
        sols = []
        qks = []
        for h in range(2):
            g1 = ex[:, (2 + h) * LANES:(3 + h) * LANES]
            diff = g1 - g1.T
            dec = jnp.where(lower, jnp.exp(jnp.where(lower, diff, 0.0)), 0.0)
            kk = kq[2 * h * CHUNK:(2 * h + 1) * CHUNK]
            sols.append(_unit_lower_solve(jnp.where(strict, kk * dec, 0.0), rhs))
            qks.append((kq[(2 * h + 1) * CHUNK:(2 * h + 2) * CHUNK] * dec).astype(BF16))
        first2 = jnp.concatenate([first, first], axis=1)
        sol = jnp.where(first2, sols[0], sols[1])
        u = sol[:, 0:LANES]
        wq = jnp.concatenate([sol[:, LANES:2 * LANES], q * eg_p], axis=0).astype(BF16)
        k_tail_t = (k * jnp.exp(g_last - gc_p)).T.astype(BF16)
        return rows, u, wq, jnp.concatenate(qks, axis=1), k_tail_t, jnp.exp(g_last)

    def advance(s, parts):
        rows, u, wq, qk2, k_tail_t, decay = parts
        ws = _dot(wq, s.astype(BF16))
        v_new = u - ws[:CHUNK]
        v_heads = jnp.concatenate([jnp.where(first, v_new, 0.0), jnp.where(first, 0.0, v_new)], axis=0)
        os_[rows, :] = ws[CHUNK:] + _dot(qk2, v_heads.astype(BF16))
        upd = _dot(k_tail_t, v_new.astype(BF16))
        return s * decay + jnp.where(same_head, upd, 0.0)

    def normalize(tt, _):
        rows = pl.ds(pl.multiple_of(tt * tile, tile), tile)
        o = os_[rows, :]
        ms = _dot_xc(o * o, head_ones) * (1.0 / GDN_V_DIM)
        zf = z_ref[0, rows, :].astype(F32)
        o_ref[0, rows, :] = (o * lax.rsqrt(ms + RMS_EPS) * nw_ref[...] * (zf * _sigmoid(zf))).astype(BF16)
        return 0

    def finish(s):
        s_ref[0, 0] = s[0:GDN_K_DIM, 0:GDN_V_DIM]
        s_ref[0, 1] = s[GDN_K_DIM:, GDN_V_DIM:]
        lax.fori_loop(0, seq // tile, normalize, 0)

    return setup, within_chunk, advance, finish


GDN_GROUP = 4


def _gdn_prompt_kernel(tile, group, *refs):
    setup, within_chunk, advance, finish = _gdn_prompt_parts(tile, pl.program_id(1), *refs)
    seq = refs[0].shape[1]
    setup()

    def chunk_group(cg, s):
        parts = [within_chunk(cg * group + i) for i in range(group)]
        for p in parts:
            s = advance(s, p)
        return s

    finish(lax.fori_loop(0, seq // (CHUNK * group), chunk_group, jnp.zeros((LANES, LANES), F32)))


def _gdn_prompt(gqkv, z, ba, conv_w, alog_lane, dtb_lane, nw_lane):
    b, seq, _ = gqkv.shape
    n_pair = GDN_HEADS // 2
    blk = lambda off: pl.BlockSpec((1, seq, LANES), lambda i, j: (i, 0, off + j))
    cw = lambda off: pl.BlockSpec((GDN_CONV, LANES), lambda i, j: (0, off + j))
    lane_vec = lambda: pl.BlockSpec((1, LANES), lambda i, j: (0, 0))
    return pl.pallas_call(
        functools.partial(_gdn_prompt_kernel, min(GDN_TILE, seq), min(GDN_GROUP, seq // CHUNK)),
        grid=(b, n_pair),
        in_specs=[blk(0), blk(n_pair), blk(2 * n_pair), blk(0),
                  pl.BlockSpec((1, seq, LANES), lambda i, j: (i, 0, 0)),
                  cw(0), cw(n_pair), cw(2 * n_pair), lane_vec(), lane_vec(), lane_vec()],
        out_specs=[blk(0), pl.BlockSpec((1, 2, GDN_K_DIM, GDN_V_DIM), lambda i, j: (i, j, 0, 0))],
        out_shape=[jax.ShapeDtypeStruct((b, seq, GDN_V_W), BF16),
                   jax.ShapeDtypeStruct((b, GDN_HEADS, GDN_K_DIM, GDN_V_DIM), F32)],
        scratch_shapes=[pltpu.VMEM((seq + 8, LANES), F32)] * 3 + [pltpu.VMEM((seq, LANES), F32)] * 3
        + [pltpu.VMEM((seq, 4 * LANES), F32), pltpu.VMEM((seq, LANES), F32)],
        compiler_params=_cparams("parallel", "parallel"),
        name="gdn_prompt",
    )(gqkv, gqkv, gqkv, z, ba, conv_w, conv_w, conv_w, alog_lane, dtb_lane, nw_lane)


GDN_BT = 8


def _gdn_sample_kernel(g_ref, c0_ref, gt_ref, c0t_ref, ba_ref, bat_ref, z_ref, s_ref, cw_ref, cwt_ref,
                       alogc_ref, dtbc_ref, nw_ref, o_ref, sn_ref, o_scr):
    scale = GDN_K_DIM ** -0.5
    y = (c0_ref[:, 0, :] * cw_ref[0:1, :] + c0_ref[:, 1, :] * cw_ref[1:2, :]
         + c0_ref[:, 2, :] * cw_ref[2:3, :] + g_ref[...] * cw_ref[3:4, :])
    v_row = (y * _sigmoid(y))[:, 2 * GDN_QK_W:]
    raw = ba_ref[...]
    er = _iota((LANES, GDN_V_W), 0)
    ec = _iota((LANES, GDN_V_W), 1)
    expand = jnp.where(er == ec // GDN_V_DIM, 1.0, 0.0).astype(BF16)
    beta_row = _dot_xc(_sigmoid(raw), expand)
    u_row = v_row * beta_row
    yt = (c0t_ref[0, 0] * cwt_ref[0] + c0t_ref[0, 1] * cwt_ref[1] + c0t_ref[0, 2] * cwt_ref[2]
          + gt_ref[0] * cwt_ref[3])
    yt = yt * _sigmoid(yt)
    rawt = bat_ref[0]
    beta_c = _sigmoid(rawt[0:GDN_HEADS])
    g_c = -jnp.exp(alogc_ref[...]) * _softplus(rawt[GDN_HEADS:] + dtbc_ref[...])
    eg_c = jnp.exp(g_c)
    for h in range(GDN_HEADS):
        qh = yt[h * GDN_K_DIM:(h + 1) * GDN_K_DIM]
        kh = yt[GDN_QK_W + h * GDN_K_DIM:GDN_QK_W + (h + 1) * GDN_K_DIM]
        qn = qh * lax.rsqrt(jnp.sum(qh * qh, axis=0, keepdims=True) + L2_EPS) * scale
        kn = kh * lax.rsqrt(jnp.sum(kh * kh, axis=0, keepdims=True) + L2_EPS)
        eg = eg_c[h:h + 1, :]
        wn = kn * (beta_c[h:h + 1, :] * eg)
        qd = qn * eg
        qk = jnp.sum(qn * kn, axis=0, keepdims=True)
        for b in range(GDN_BT):
            s = s_ref[b, h]
            u = u_row[b:b + 1, h * GDN_V_DIM:(h + 1) * GDN_V_DIM]
            v_new = u - jnp.sum(wn[:, b:b + 1] * s, axis=0, keepdims=True)
            o = jnp.sum(qd[:, b:b + 1] * s, axis=0, keepdims=True) + qk[:, b:b + 1] * v_new
            sn_ref[b, h] = s * eg[:, b:b + 1] + kn[:, b:b + 1] * v_new
            o_scr[b:b + 1, h * GDN_V_DIM:(h + 1) * GDN_V_DIM] = o
    o = o_scr[...]
    r = _iota((GDN_V_W, GDN_V_W), 0) // GDN_V_DIM
    c = _iota((GDN_V_W, GDN_V_W), 1) // GDN_V_DIM
    ms = _dot_xc(o * o, jnp.where(r == c, 1.0, 0.0).astype(BF16)) * (1.0 / GDN_V_DIM)
    zf = z_ref[...].astype(F32)
    o_ref[...] = (o * lax.rsqrt(ms + RMS_EPS) * nw_ref[...] * (zf * _sigmoid(zf))).astype(BF16)


def _gdn_sample(gqkv, conv0, ba, z, state, conv_w, alog, dtb, norm_w):
    db = gqkv.shape[0]
    nt = db // GDN_BT
    qk_w = 2 * GDN_QK_W
    gt = gqkv[:, :qk_w].T.reshape(qk_w, nt, GDN_BT).transpose(1, 0, 2)
    c0t = conv0[:, :, :qk_w].transpose(1, 2, 0).reshape(3, qk_w, nt, GDN_BT).transpose(2, 0, 1, 3)
    bat = ba[:, :2 * GDN_HEADS].T.reshape(2 * GDN_HEADS, nt, GDN_BT).transpose(1, 0, 2)
    cwt = conv_w[:, :qk_w].reshape(GDN_CONV, qk_w, 1)
    alog_c = alog.astype(F32).reshape(GDN_HEADS, 1)
    dtb_c = dtb.astype(F32).reshape(GDN_HEADS, 1)
    nw = jnp.tile(norm_w.astype(F32), GDN_HEADS).reshape(1, GDN_V_W)
    full = lambda shape: pl.BlockSpec(shape, lambda i: (0,) * len(shape))
    return pl.pallas_call(
        _gdn_sample_kernel,
        grid=(nt,),
        in_specs=[pl.BlockSpec((GDN_BT, GDN_CONV_W), lambda i: (i, 0)),
                  pl.BlockSpec((GDN_BT, GDN_CONV - 1, GDN_CONV_W), lambda i: (i, 0, 0)),
                  pl.BlockSpec((1, qk_w, GDN_BT), lambda i: (i, 0, 0)),
                  pl.BlockSpec((1, GDN_CONV - 1, qk_w, GDN_BT), lambda i: (i, 0, 0, 0)),
                  pl.BlockSpec((GDN_BT, LANES), lambda i: (i, 0)),
                  pl.BlockSpec((1, 2 * GDN_HEADS, GDN_BT), lambda i: (i, 0, 0)),
                  pl.BlockSpec((GDN_BT, GDN_V_W), lambda i: (i, 0)),
                  pl.BlockSpec((GDN_BT, GDN_HEADS, GDN_K_DIM, GDN_V_DIM), lambda i: (i, 0, 0, 0)),
                  full((GDN_CONV, GDN_CONV_W)), full((GDN_CONV, qk_w, 1)),
                  full((GDN_HEADS, 1)), full((GDN_HEADS, 1)), full((1, GDN_V_W))],
        out_specs=[pl.BlockSpec((GDN_BT, GDN_V_W), lambda i: (i, 0)),
                   pl.BlockSpec((GDN_BT, GDN_HEADS, GDN_K_DIM, GDN_V_DIM), lambda i: (i, 0, 0, 0))],
        out_shape=[jax.ShapeDtypeStruct((db, GDN_V_W), BF16),
                   jax.ShapeDtypeStruct(state.shape, F32)],
        scratch_shapes=[pltpu.VMEM((GDN_BT, GDN_V_W), F32)],
        compiler_params=_cparams("parallel"),
        name="gdn_sample",
    )(gqkv, conv0, gt, c0t, ba, bat, z, state, conv_w, cwt, alog_c, dtb_c, nw)


def _mem_prompt_kernel(q_ref, k_ref, v_ref, o_ref):
    scale = MEM_HEAD_DIM ** -0.5
    for h in range(MEM_HEADS):
        lanes = slice(h * MEM_HEAD_DIM, (h + 1) * MEM_HEAD_DIM)
        s = _dot_nt(q_ref[0, :, lanes], k_ref[0, :, lanes].astype(BF16)) * scale
        m = jnp.max(s, axis=-1, keepdims=True)
        e = jnp.exp(s - m)
        p = e / jnp.sum(e, axis=-1, keepdims=True)
        o_ref[0, :, lanes] = _dot(p.astype(BF16), v_ref[0, :, lanes].astype(BF16)).astype(BF16)


def _mem_prompt(q, mem_k, mem_v, tq):
    b, seq, _ = q.shape
    return pl.pallas_call(
        _mem_prompt_kernel,
        grid=(b, seq // tq),
        in_specs=[pl.BlockSpec((1, tq, MEM_W), lambda i, j: (i, j, 0)),
                  pl.BlockSpec((1, N_MEM, MEM_W), lambda i, j: (i, 0, 0)),
                  pl.BlockSpec((1, N_MEM, MEM_W), lambda i, j: (i, 0, 0))],
        out_specs=pl.BlockSpec((1, tq, MEM_W), lambda i, j: (i, j, 0)),
        out_shape=jax.ShapeDtypeStruct((b, seq, MEM_W), BF16),
        compiler_params=_cparams("parallel", "parallel"),
        name="mem_prompt",
    )(q, mem_k, mem_v)


MEM_BT = 4


def _mem_sample_kernel(q_ref, k_ref, v_ref, o_ref):
    scale = MEM_HEAD_DIM ** -0.5
    rows = 8
    head_of_lane = _iota((rows, MEM_W), 1) // MEM_HEAD_DIM
    own = head_of_lane == _iota((rows, MEM_W), 0)
    heads = range(MEM_HEADS)
    for b in range(MEM_BT):
        qm = jnp.where(own, q_ref[b].astype(F32), 0.0).astype(BF16)
        kb = jnp.concatenate([k_ref[b, :, h, :] for h in heads], axis=1).astype(BF16)
        vb = jnp.concatenate([v_ref[b, :, h, :] for h in heads], axis=1).astype(BF16)
        s = _dot_nt(qm, kb) * scale
        m = jnp.max(s, axis=-1, keepdims=True)
        e = jnp.exp(s - m)
        p = e / jnp.sum(e, axis=-1, keepdims=True)
        o = _dot(p.astype(BF16), vb)
        o_ref[b] = jnp.sum(jnp.where(own, o, 0.0), axis=0, keepdims=True).astype(BF16)


def _mem_sample(q, mem_k, mem_v):
    db = q.shape[0]
    out = pl.pallas_call(
        _mem_sample_kernel,
        grid=(db // MEM_BT,),
        in_specs=[pl.BlockSpec((MEM_BT, 1, MEM_W), lambda i: (i, 0, 0)),
                  pl.BlockSpec((MEM_BT, N_MEM, MEM_HEADS, MEM_HEAD_DIM), lambda i: (i, 0, 0, 0)),
                  pl.BlockSpec((MEM_BT, N_MEM, MEM_HEADS, MEM_HEAD_DIM), lambda i: (i, 0, 0, 0))],
        out_specs=pl.BlockSpec((MEM_BT, 1, MEM_W), lambda i: (i, 0, 0)),
        out_shape=jax.ShapeDtypeStruct((db, 1, MEM_W), BF16),
        compiler_params=_cparams("parallel"),
        name="mem_sample",
    )(q.reshape(db, 1, MEM_W), mem_k, mem_v)
    return out.reshape(db, MEM_W)


def _merge_kernel(x_ref, osb_ref, ogdn_ref, omem_ref, wg_ref, wsb_ref, wgdn_ref, wmem_ref, wo_ref, g_ref, b_ref,
                  h_ref):
    x = x_ref[...]
    xb = x.astype(BF16)
    merged = None
    for k, (o_ref, w_ref) in enumerate(((osb_ref, wsb_ref), (ogdn_ref, wgdn_ref), (omem_ref, wmem_ref))):
        gate = _sigmoid(_dot(xb, wg_ref[:, k * D_MODEL:(k + 1) * D_MODEL]))
        term = gate * _dot(o_ref[...], w_ref[...])
        merged = term if merged is None else merged + term
    pre = DN_ALPHA * x + _dot(merged.astype(BF16), wo_ref[...])
    h_ref[...] = _layer_norm(pre, g_ref[...], b_ref[...])


def _merge_ln(x2d, o_sb, o_gdn, o_mem, w_gates, w_sb, w_gdn, w_mem, w_o, g, b, tm):
    t = x2d.shape[0]
    row = lambda w: pl.BlockSpec((tm, w), lambda i: (i, 0))
    return pl.pallas_call(
        _merge_kernel,
        grid=(t // tm,),
        in_specs=[row(D_MODEL), row(512), row(512), row(512),
                  _const_spec((D_MODEL, 3 * D_MODEL)), _const_spec((512, D_MODEL)), _const_spec((512, D_MODEL)),
                  _const_spec((512, D_MODEL)), _const_spec((D_MODEL, D_MODEL)),
                  _const_spec((1, D_MODEL)), _const_spec((1, D_MODEL))],
        out_specs=row(D_MODEL),
        out_shape=jax.ShapeDtypeStruct((t, D_MODEL), F32),
        compiler_params=_cparams("parallel"),
        name="merge_ln",
    )(x2d, o_sb, o_gdn, o_mem, w_gates, w_sb, w_gdn, w_mem, w_o, g, b)


ROUTE_TM = 128
GROUP_SIZE = N_EXPERTS // N_GROUPS


def _first_max(x, idx, n):
    m = jnp.max(x, axis=0, keepdims=True)
    first = jnp.min(jnp.where(x == m, idx, n), axis=0, keepdims=True)
    return m, first


U32 = jnp.uint32
HALF = D_MODEL // 2


def _pack_rows(x):
    bits = pltpu.bitcast(x.astype(BF16).astype(F32), U32)
    return bits[:, :HALF] | (bits[:, HALF:] >> 16)


def _unpack_rows(w):
    return pltpu.bitcast(w & U32(0xFFFF0000), F32), pltpu.bitcast(w << 16, F32)


def _two_group_specs(tm, n_first, width):
    return [pl.BlockSpec((tm, width), lambda i: (jnp.minimum(i, n_first - 1), 0)),
            pl.BlockSpec((tm, width), lambda i: (jnp.maximum(i - n_first, 0), 0))]


def _route_kernel(n_first, hp_ref, hs_ref, wr_ref, bias_ref, e_ref, rank_ref, w_ref, cnt_ref, pk_ref, carry):
    i = pl.program_id(0)

    @pl.when(i == 0)
    def _():
        carry[...] = jnp.zeros_like(carry)

    tm = hp_ref.shape[0]
    neg = -jnp.inf
    h = jnp.where(i < n_first, hp_ref[...], hs_ref[...])
    pk_ref[...] = _pack_rows(h)
    hh, hl = _split2(h)
    wh, wl = _split2(wr_ref[...])
    logits = _dot_nt(wh, hh) + _dot_nt(wh, hl) + _dot_nt(wl, hh)
    scores = _sigmoid(logits)
    biased = scores + bias_ref[...]
    grp = []
    sub = _iota((GROUP_SIZE, tm), 0)
    for g in range(N_GROUPS):
        xg = biased[g * GROUP_SIZE:(g + 1) * GROUP_SIZE]
        m1, f1 = _first_max(xg, sub, GROUP_SIZE)
        m2 = jnp.max(jnp.where(sub == f1, neg, xg), axis=0, keepdims=True)
        grp.append(m1 + m2)
    gs = jnp.concatenate(grp, axis=0)
    gidx = _iota((N_GROUPS, tm), 0)
    gsel = gidx < 0
    for _ in range(TOPK_GROUPS):
        _, f = _first_max(gs, gidx, N_GROUPS)
        hit = gidx == f
        gsel = jnp.logical_or(gsel, hit)
        gs = jnp.where(hit, neg, gs)
    eidx = _iota((N_EXPERTS, tm), 0)
    gsel_f = jnp.where(gsel, 1.0, 0.0)
    allowed = jnp.concatenate(
        [jnp.broadcast_to(gsel_f[g:g + 1], (GROUP_SIZE, tm)) for g in range(N_GROUPS)], axis=0) > 0.5
    choice = jnp.where(allowed, biased, neg)
    picks = []
    weights = []
    chosen = jnp.zeros((N_EXPERTS, tm), F32)
    for _ in range(TOP_K):
        _, f = _first_max(choice, eidx, N_EXPERTS)
        hit = eidx == f
        picks.append(f)
        weights.append(jnp.sum(jnp.where(hit, scores, 0.0), axis=0, keepdims=True))
        chosen = jnp.where(hit, 1.0, chosen)
        choice = jnp.where(hit, neg, choice)
    wsum = weights[0]
    for w in weights[1:]:
        wsum = wsum + w
    tr = _iota((tm, tm), 0)
    tc = _iota((tm, tm), 1)
    before = jnp.where(tr < tc, 1.0, 0.0).astype(BF16)
    chosen_b = chosen.astype(BF16)
    rank = _dot(chosen_b, before) + carry[...]
    ranks = [jnp.sum(jnp.where(eidx == f, rank, 0.0), axis=0, keepdims=True) for f in picks]
    e_ref[...] = jnp.concatenate(picks, axis=0)
    rank_ref[...] = jnp.concatenate(ranks, axis=0).astype(I32)
    w_ref[...] = jnp.concatenate(weights, axis=0) / wsum * ROUTE_SCALE
    carry[...] = carry[...] + _dot(chosen_b, jnp.ones((tm, LANES), BF16))
    cnt_ref[...] = carry[...]


def _route(h_first, h_second, w_router_t, bias_col):
    tm = ROUTE_TM
    n_first = h_first.shape[0] // tm
    t = h_first.shape[0] + h_second.shape[0]
    col = lambda: pl.BlockSpec((TOP_K, tm), lambda i: (0, i))
    return pl.pallas_call(
        functools.partial(_route_kernel, n_first),
        grid=(t // tm,),
        in_specs=_two_group_specs(tm, n_first, D_MODEL)
        + [pl.BlockSpec((N_EXPERTS, D_MODEL), lambda i: (0, 0)),
           pl.BlockSpec((N_EXPERTS, 1), lambda i: (0, 0))],
        out_specs=[col(), col(), col(), pl.BlockSpec((N_EXPERTS, LANES), lambda i: (0, 0)),
                   pl.BlockSpec((tm, HALF), lambda i: (i, 0))],
        out_shape=[jax.ShapeDtypeStruct((TOP_K, t), I32), jax.ShapeDtypeStruct((TOP_K, t), I32),
                   jax.ShapeDtypeStruct((TOP_K, t), F32), jax.ShapeDtypeStruct((N_EXPERTS, LANES), F32),
                   jax.ShapeDtypeStruct((t, HALF), U32)],
        scratch_shapes=[pltpu.VMEM((N_EXPERTS, LANES), F32)],
        compiler_params=_cparams("arbitrary"),
        name="route",
    )(h_first, h_second, w_router_t, bias_col)


DISPATCH_TM = 128


def _row_copy(src, dst, s_row, d_row, sem):
    return pltpu.make_async_copy(src.at[pl.ds(s_row, 1)], dst.at[pl.ds(d_row, 1)], sem)


def _dispatch_kernel(dest_ref, h_ref, xs_ref, sem):
    def issue(t, _):
        for r in range(TOP_K):
            _row_copy(h_ref, xs_ref, t, dest_ref[t * TOP_K + r], sem).start(priority=r % 2)
        return 0

    lax.fori_loop(0, DISPATCH_TM, issue, 0)

    def drain(t, _):
        for r in range(TOP_K):
            _row_copy(h_ref, xs_ref, 0, 0, sem).wait()
        return 0

    lax.fori_loop(0, DISPATCH_TM, drain, 0)


def _dispatch(h_packed, dest_flat):
    t = h_packed.shape[0]
    return pl.pallas_call(
        _dispatch_kernel,
        grid=(t // DISPATCH_TM,),
        in_specs=[pl.BlockSpec((DISPATCH_TM * TOP_K,), lambda i: (i,), memory_space=pltpu.SMEM),
                  pl.BlockSpec((DISPATCH_TM, HALF), lambda i: (i, 0))],
        out_specs=pl.BlockSpec(memory_space=pl.ANY),
        out_shape=jax.ShapeDtypeStruct((t * TOP_K, HALF), U32),
        scratch_shapes=[pltpu.SemaphoreType.DMA(())],
        compiler_params=_cparams("arbitrary"),
        name="moe_dispatch",
    )(dest_flat, h_packed)


def _expert_kernel(blk_ref, exp_ref, lo_ref, hi_ref, first_ref, x_ref, wg_ref, wu_ref, wd_ref, y_ref, wgu, wdn):
    del blk_ref
    i = pl.program_id(0)

    @pl.when((i == 0) | (exp_ref[i] != exp_ref[jnp.maximum(i - 1, 0)]))
    def _():
        wgu[:, 0:D_EXPERT] = wg_ref[0].astype(BF16)
        wgu[:, D_EXPERT:] = wu_ref[0].astype(BF16)
        wdn[...] = wd_ref[0].astype(BF16)

    x = jnp.concatenate(_unpack_rows(x_ref[...]), axis=1).astype(BF16)
    gu = _dot(x, wgu[...])
    gate = gu[:, 0:D_EXPERT]
    act = gate * _sigmoid(gate) * gu[:, D_EXPERT:]
    y = _pack_rows(_dot(act.astype(BF16), wdn[...]))
    row = _iota((ROW_BLOCK, 1), 0)
    mine = (row >= lo_ref[i]) & (row < hi_ref[i])

    @pl.when(first_ref[i] == 1)
    def _():
        y_ref[...] = jnp.where(mine, y, U32(0))

    @pl.when(first_ref[i] == 0)
    def _():
        y_ref[...] = jnp.where(mine, y, y_ref[...])


def _expert_mm(xs, w_gate, w_up, w_down, blk, exp, lo, hi, first):
    n_items = blk.shape[0]
    grid_spec = pltpu.PrefetchScalarGridSpec(
        num_scalar_prefetch=5,
        grid=(n_items,),
        in_specs=[pl.BlockSpec((ROW_BLOCK, HALF), lambda i, blk, exp, lo, hi, fi: (blk[i], 0)),
                  pl.BlockSpec((1, D_MODEL, D_EXPERT), lambda i, blk, exp, lo, hi, fi: (exp[i], 0, 0)),
                  pl.BlockSpec((1, D_MODEL, D_EXPERT), lambda i, blk, exp, lo, hi, fi: (exp[i], 0, 0)),
                  pl.BlockSpec((1, D_EXPERT, D_MODEL), lambda i, blk, exp, lo, hi, fi: (exp[i], 0, 0))],
        out_specs=pl.BlockSpec((ROW_BLOCK, HALF), lambda i, blk, exp, lo, hi, fi: (blk[i], 0)),
        scratch_shapes=[pltpu.VMEM((D_MODEL, 2 * D_EXPERT), BF16), pltpu.VMEM((D_EXPERT, D_MODEL), BF16)],
    )
    return pl.pallas_call(
        _expert_kernel,
        grid_spec=grid_spec,
        out_shape=jax.ShapeDtypeStruct(xs.shape, U32),
        compiler_params=_cparams("arbitrary"),
        name="moe_experts",
    )(blk, exp, lo, hi, first, xs, w_gate, w_up, w_down)


COMBINE_TM = 128


def _combine_kernel(n_first, dest_ref, hp_ref, hs_ref, wt_ref, y_ref, wsg_ref, wsd_ref, g_ref, b_ref,
                    op_ref, os_ref, buf, sem):
    i = pl.program_id(0)

    def issue(t, _):
        for r in range(TOP_K):
            pltpu.make_async_copy(y_ref.at[pl.ds(dest_ref[t * TOP_K + r], 1)], buf.at[r, pl.ds(t, 1)],
                                  sem).start(priority=r % 2)
        return 0

    lax.fori_loop(0, COMBINE_TM, issue, 0)
    h = jnp.where(i < n_first, hp_ref[...], hs_ref[...])
    gu = _dot(h.astype(BF16), wsg_ref[...])
    gate = gu[:, 0:D_SHARED]
    f = _dot((gate * _sigmoid(gate) * gu[:, D_SHARED:]).astype(BF16), wsd_ref[...])

    def drain(t, _):
        for r in range(TOP_K):
            pltpu.make_async_copy(y_ref.at[pl.ds(0, 1)], buf.at[r, pl.ds(t, 1)], sem).wait()
        return 0

    lax.fori_loop(0, COMBINE_TM, drain, 0)
    lo = jnp.zeros((COMBINE_TM, HALF), F32)
    hi = jnp.zeros((COMBINE_TM, HALF), F32)
    for r in range(TOP_K):
        ya, yb = _unpack_rows(buf[r])
        w = wt_ref[:, r:r + 1]
        lo = lo + w * ya
        hi = hi + w * yb
    out = _layer_norm(DN_ALPHA * h + f + jnp.concatenate([lo, hi], axis=1), g_ref[...], b_ref[...])

    @pl.when(i < n_first)
    def _():
        op_ref[...] = out

    @pl.when(i >= n_first)
    def _():
        os_ref[...] = out


def _combine(h_first, h_second, w_tok, dest_flat, y, w_sgu, w_sd, g, b):
    tm = COMBINE_TM
    n_first = h_first.shape[0] // tm
    t = h_first.shape[0] + h_second.shape[0]
    return pl.pallas_call(
        functools.partial(_combine_kernel, n_first),
        grid=(t // tm,),
        in_specs=[pl.BlockSpec((tm * TOP_K,), lambda i: (i,), memory_space=pltpu.SMEM)]
        + _two_group_specs(tm, n_first, D_MODEL)
        + [pl.BlockSpec((tm, TOP_K), lambda i: (i, 0)),
           pl.BlockSpec(memory_space=pl.ANY),
           _const_spec((D_MODEL, 2 * D_SHARED)), _const_spec((D_SHARED, D_MODEL)),
           _const_spec((1, D_MODEL)), _const_spec((1, D_MODEL))],
        out_specs=_two_group_specs(tm, n_first, D_MODEL),
        out_shape=[jax.ShapeDtypeStruct(h_first.shape, F32), jax.ShapeDtypeStruct(h_second.shape, F32)],
        scratch_shapes=[pltpu.VMEM((TOP_K, tm, HALF), U32), pltpu.SemaphoreType.DMA(())],
        compiler_params=_cparams("arbitrary"),
        name="moe_combine",
    )(dest_flat, h_first, h_second, w_tok, y, w_sgu, w_sd, g, b)


def _moe_plan(counts, n_rows):
    n_blocks = n_rows // ROW_BLOCK
    n_items = n_blocks + N_EXPERTS
    start = jnp.cumsum(counts) - counts
    end = start + counts
    first_blk = start // ROW_BLOCK
    last_blk = jnp.maximum(end - 1, start) // ROW_BLOCK
    nb = jnp.where(counts > 0, last_blk - first_blk + 1, 0)
    item_end = jnp.cumsum(nb)
    item_start = item_end - nb
    total = item_end[-1]
    idx = jnp.arange(n_items, dtype=I32)
    last_item = jnp.maximum(total - 1, 0)
    src = jnp.minimum(idx, last_item)
    exp = jnp.minimum(jnp.sum(item_end[None, :] <= src[:, None], axis=1), N_EXPERTS - 1).astype(I32)
    onehot = exp[:, None] == jnp.arange(N_EXPERTS, dtype=I32)[None, :]
    take = lambda table: jnp.sum(jnp.where(onehot, table[None, :], 0), axis=1)
    blk = (take(first_blk) + src - take(item_start)).astype(I32)
    valid = idx < total
    lo = jnp.clip(take(start) - blk * ROW_BLOCK, 0, ROW_BLOCK)
    hi = jnp.clip(take(end) - blk * ROW_BLOCK, 0, ROW_BLOCK)
    first = (valid & (lo == 0)).astype(I32)
    lo = jnp.where(valid, lo, 0).astype(I32)
    hi = jnp.where(valid, hi, 0).astype(I32)
    return start, blk, exp, lo, hi, first


def _moe(h_first, h_second, w_router, router_bias, w_exp_gate, w_exp_up, w_exp_down, w_sgu, w_sd, g, b):
    t = h_first.shape[0] + h_second.shape[0]
    picks, ranks, weights, cnt, h_packed = _route(h_first, h_second, w_router.T,
                                                  router_bias.astype(F32).reshape(N_EXPERTS, 1))
    counts = cnt[:, 0].astype(I32)
    start, blk, exp, lo, hi, first = _moe_plan(counts, t * TOP_K)
    experts = jnp.arange(N_EXPERTS, dtype=I32)[:, None, None]
    dest = ranks + jnp.sum(jnp.where(picks[None] == experts, start.astype(I32)[:, None, None], 0), axis=0)
    dest_flat = dest.T.reshape(t * TOP_K)
    xs = _dispatch(h_packed, dest_flat)
    y = _expert_mm(xs, w_exp_gate, w_exp_up, w_exp_down, blk, exp, lo, hi, first)
    return _combine(h_first, h_second, weights.T, dest_flat, y, w_sgu, w_sd, g, b)


def kernel(x_prompt, x_sample, cache_sb_k, cache_sb_v, page_table, state_gdn, state_gdn_conv, cache_mem_k, cache_mem_v, mem_prompt, w_in, sb_logit_bias, conv_w, gdn_A_log, gdn_dt_bias, gdn_norm_w, w_mem_k, w_mem_v, w_br_sb, w_br_gdn, w_br_mem, w_o, ln1_g, ln1_b, w_router, router_bias, w_exp_gate, w_exp_up, w_exp_down, w_sh_gate, w_sh_up, w_sh_down, ln2_g, ln2_b):
    bsz, seq, _ = x_prompt.shape
    db = x_sample.shape[0]
    tp = bsz * seq

    ba_off = 3 * SB_W + GDN_CONV_W + GDN_V_W
    mq_off = ba_off + 2 * GDN_HEADS
    gate_off = mq_off + MEM_W
    w_main = jnp.concatenate([w_in[:, :SB_W], w_in[:, 3 * SB_W:ba_off], w_in[:, mq_off:gate_off]],
                             axis=1).astype(BF16)
    w_kvt = w_in[:, SB_W:3 * SB_W].T.astype(BF16)
    w_ba = jnp.pad(w_in[:, ba_off:mq_off], ((0, 0), (0, LANES - 2 * GDN_HEADS))).astype(BF16)
    w_gates = w_in[:, gate_off:].astype(BF16)
    w_kv = jnp.concatenate([w_mem_k, w_mem_v], axis=1).astype(BF16)
    w_sb, w_gdn, w_mem, w_ob = (w.astype(BF16) for w in (w_br_sb, w_br_gdn, w_br_mem, w_o))
    w_sgu = jnp.concatenate([w_sh_gate, w_sh_up], axis=1).astype(BF16)
    w_sd = w_sh_down.astype(BF16)
    row = lambda v: v.astype(F32).reshape(1, -1)
    head_lane = lambda v: jnp.pad(v.astype(F32), (GDN_HEADS, LANES - 2 * GDN_HEADS)).reshape(1, LANES)
    nw_pair = jnp.tile(gdn_norm_w.astype(F32), 2).reshape(1, LANES)
    bias = sb_logit_bias.astype(F32)

    xp = x_prompt.reshape(tp, D_MODEL)
    q, kt, vt, gqkv, z, mq, ba = _proj_in(xp, seq, w_main, w_kvt, w_ba, min(512, seq))
    mem_k, mem_v = _mem_kv(mem_prompt.reshape(bsz * N_MEM, D_MODEL), w_kv, 512)
    sh = lambda a: a.reshape(bsz, seq, a.shape[-1])
    o_sb = _sb_prompt(sh(q), kt, vt, bias)
    gq3 = sh(gqkv)
    o_gdn, s_p = _gdn_prompt(gq3, sh(z), sh(ba), conv_w.astype(F32), head_lane(gdn_A_log),
                             head_lane(gdn_dt_bias), nw_pair)
    o_mem = _mem_prompt(sh(mq), mem_k.reshape(bsz, N_MEM, MEM_W), mem_v.reshape(bsz, N_MEM, MEM_W), min(512, seq))
    h_p = _merge_ln(xp, o_sb.reshape(tp, SB_W), o_gdn.reshape(tp, GDN_V_W), o_mem.reshape(tp, MEM_W),
                    w_gates, w_sb, w_gdn, w_mem, w_ob, row(ln1_g), row(ln1_b), 512)

    xs = x_sample.reshape(db, D_MODEL)
    q_s, kt_s, vt_s, gqkv_s, z_s, mq_s, ba_s = _proj_in(xs, db, w_main, w_kvt, w_ba, db)
    o_sb_s = _sb_sample(q_s, cache_sb_k, cache_sb_v, page_table, bias)
    o_gdn_s, s_s = _gdn_sample(gqkv_s, state_gdn_conv.astype(F32), ba_s, z_s, state_gdn.astype(F32),
                               conv_w.astype(F32), gdn_A_log, gdn_dt_bias, gdn_norm_w)
    o_mem_s = _mem_sample(mq_s, cache_mem_k, cache_mem_v)
    h_s = _merge_ln(xs, o_sb_s, o_gdn_s, o_mem_s, w_gates, w_sb, w_gdn, w_mem, w_ob, row(ln1_g), row(ln1_b), db)

    y_p, y_s = _moe(h_p, h_s, w_router, router_bias, w_exp_gate, w_exp_up, w_exp_down,
                    w_sgu, w_sd, row(ln2_g), row(ln2_b))

    heads_p = lambda a: a.reshape(bsz, SB_HEADS, SB_HEAD_DIM, seq).transpose(0, 3, 1, 2)
    heads_s = lambda a: a[0].T.reshape(db, 1, SB_HEADS, SB_HEAD_DIM)
    return (y_p.reshape(bsz, seq, D_MODEL), y_s.reshape(db, 1, D_MODEL),
            heads_p(kt), heads_p(vt),
            s_p, gq3[:, seq - (GDN_CONV - 1):, :],
            mem_k.reshape(bsz, N_MEM, MEM_HEADS, MEM_HEAD_DIM), mem_v.reshape(bsz, N_MEM, MEM_HEADS, MEM_HEAD_DIM),
            heads_s(kt_s), heads_s(vt_s),
            s_s, jnp.concatenate([state_gdn_conv[:, 1:, :], gqkv_s[:, None, :]], axis=1))
```

```python
import functools

import jax
import jax.numpy as jnp
from jax import lax
from jax.experimental import pallas as pl
from jax.experimental.pallas import tpu as pltpu

F32 = jnp.float32
BF16 = jnp.bfloat16
I32 = jnp.int32

D_MODEL = 1024
PAGE_SIZE = 128
SB_HEADS = 8
SB_HEAD_DIM = 64
SB_W = SB_HEADS * SB_HEAD_DIM
GDN_HEADS = 8
GDN_K_DIM = 64
GDN_V_DIM = 64
GDN_CONV = 4
GDN_QK_W = GDN_HEADS * GDN_K_DIM
GDN_V_W = GDN_HEADS * GDN_V_DIM
GDN_CONV_W = 2 * GDN_QK_W + GDN_V_W
N_MEM = 256
MEM_HEADS = 4
MEM_HEAD_DIM = 128
MEM_W = MEM_HEADS * MEM_HEAD_DIM
N_EXPERTS = 64
TOP_K = 8
N_GROUPS = 8
TOPK_GROUPS = 4
D_EXPERT = 256
D_SHARED = 256
ROUTE_SCALE = 2.5
DEPTH = 1
DN_ALPHA = (2 * DEPTH) ** 0.25
LN_EPS = 1e-5
RMS_EPS = 1e-6
L2_EPS = 1e-6

LANES = 128
CHUNK = 128
ROW_BLOCK = 512
VMEM_LIMIT = 56 * 1024 * 1024


def _cparams(*sem):
    return pltpu.CompilerParams(dimension_semantics=sem, vmem_limit_bytes=VMEM_LIMIT)


def _dot(a, b):
    return jnp.dot(a, b, preferred_element_type=F32)


def _dot_nt(a, b):
    return lax.dot_general(a, b, (((1,), (1,)), ((), ())), preferred_element_type=F32)


def _split2(x):
    hi = x.astype(BF16)
    lo = (x - hi.astype(F32)).astype(BF16)
    return hi, lo


def _split3(x):
    hi = x.astype(BF16)
    r = x - hi.astype(F32)
    mid = r.astype(BF16)
    lo = (r - mid.astype(F32)).astype(BF16)
    return hi, mid, lo


def _dot_xc(x, m):
    return _dot(jnp.concatenate(_split3(x), axis=1), jnp.concatenate([m, m, m], axis=0))


def _dot_cx(m, x):
    return _dot(jnp.concatenate([m, m, m], axis=1), jnp.concatenate(_split3(x), axis=0))


def _dot_ff(a, b):
    ah, al = _split2(a)
    bh, bl = _split2(b)
    return _dot(jnp.concatenate([ah, ah, al], axis=1), jnp.concatenate([bh, bl, bh], axis=0))


def _sigmoid(x):
    return 1.0 / (1.0 + jnp.exp(-x))


def _softplus(x):
    return jnp.maximum(x, 0.0) + jnp.log(1.0 + jnp.exp(-jnp.abs(x)))


def _iota(shape, dim):
    return lax.broadcasted_iota(I32, shape, dim)


def _const_spec(shape):
    nd = len(shape)
    return pl.BlockSpec(shape, lambda *_: (0,) * nd, pipeline_mode=pl.Buffered(1))


def _layer_norm(x, g, b):
    mu = jnp.mean(x, axis=-1, keepdims=True)
    xc = x - mu
    var = jnp.mean(xc * xc, axis=-1, keepdims=True)
    return xc * lax.rsqrt(var + LN_EPS) * g + b


def _proj_kernel(x_ref, w_ref, wkv_ref, wba_ref, q_ref, kt_ref, vt_ref, g_ref, z_ref, mq_ref, ba_ref):
    xb = x_ref[...].astype(BF16)
    q_ref[...] = (_dot(xb, w_ref[:, 0:512]) * 0.125).astype(BF16)
    kvt = _dot_nt(wkv_ref[...], xb)
    kt_ref[0] = kvt[0:SB_W]
    vt_ref[0] = kvt[SB_W:]
    for c in range(3):
        g_ref[:, c * 512:(c + 1) * 512] = _dot(xb, w_ref[:, 512 + c * 512:1024 + c * 512])
    z_ref[...] = _dot(xb, w_ref[:, 2048:2560]).astype(BF16)
    mq_ref[...] = _dot(xb, w_ref[:, 2560:3072]).astype(BF16)
    ba_ref[...] = _dot(xb, wba_ref[...])


def _proj_in(x2d, seq, w_main, w_kvt, w_ba, tm):
    t = x2d.shape[0]
    per_seq = seq // tm
    row = lambda w: pl.BlockSpec((tm, w), lambda i: (i, 0))
    tr = lambda: pl.BlockSpec((1, SB_W, tm), lambda i: (i // per_seq, 0, i % per_seq))
    return pl.pallas_call(
        _proj_kernel,
        grid=(t // tm,),
        in_specs=[row(D_MODEL), _const_spec((D_MODEL, 3072)), _const_spec((2 * SB_W, D_MODEL)),
                  _const_spec((D_MODEL, LANES))],
        out_specs=[row(512), tr(), tr(), row(1536), row(512), row(512), row(LANES)],
        out_shape=[jax.ShapeDtypeStruct((t, 512), BF16), jax.ShapeDtypeStruct((t // seq, SB_W, seq), F32),
                   jax.ShapeDtypeStruct((t // seq, SB_W, seq), F32), jax.ShapeDtypeStruct((t, 1536), F32),
                   jax.ShapeDtypeStruct((t, 512), BF16), jax.ShapeDtypeStruct((t, 512), BF16),
                   jax.ShapeDtypeStruct((t, LANES), F32)],
        compiler_params=_cparams("parallel"),
        name="proj_in",
    )(x2d, w_main, w_kvt, w_ba)


def _memkv_kernel(x_ref, w_ref, k_ref, v_ref):
    xb = x_ref[...].astype(BF16)
    k_ref[...] = _dot(xb, w_ref[:, 0:512])
    v_ref[...] = _dot(xb, w_ref[:, 512:1024])


def _mem_kv(x2d, w_kv, tm):
    t = x2d.shape[0]
    row = lambda w: pl.BlockSpec((tm, w), lambda i: (i, 0))
    return pl.pallas_call(
        _memkv_kernel,
        grid=(t // tm,),
        in_specs=[row(D_MODEL), _const_spec((D_MODEL, 1024))],
        out_specs=[row(512), row(512)],
        out_shape=[jax.ShapeDtypeStruct((t, 512), F32)] * 2,
        compiler_params=_cparams("parallel"),
        name="mem_kv",
    )(x2d, w_kv)


def _cumsum_matrix():
    r = _iota((2 * CHUNK, 2 * CHUNK), 0) % CHUNK
    c = _iota((2 * CHUNK, 2 * CHUNK), 1)
    return jnp.where((c >= CHUNK) | (r > c), 1.0, 0.0).astype(BF16)


SB_QT = 512


SB_KB = 2


def _sb_prompt_kernel(qt, bias_ref, q_ref, k_ref, v_ref, o_ref, kms, vms, carry, acc):
    hp = pl.program_id(1)
    seq = q_ref.shape[1]
    nblk = seq // CHUNK
    span = SB_KB * CHUNK
    first = _iota((LANES, CHUNK), 0) < SB_HEAD_DIM
    tt = _cumsum_matrix()
    dist = _iota((qt, LANES), 0) - _iota((qt, LANES), 1)
    biases = (bias_ref[2 * hp], bias_ref[2 * hp + 1])

    def stage(i, _):
        cols = pl.ds(pl.multiple_of(i * CHUNK, CHUNK), CHUNK)
        base = pl.multiple_of(i * 2 * CHUNK, 2 * CHUNK)
        for src, dst in ((k_ref, kms), (v_ref, vms)):
            x = src[0, :, cols]
            dst[:, pl.ds(base, CHUNK)] = jnp.where(first, x, 0.0).astype(BF16)
            dst[:, pl.ds(base + CHUNK, CHUNK)] = jnp.where(first, 0.0, x).astype(BF16)
        zero = jnp.zeros((CHUNK, LANES), F32)
        carry[0, cols, :] = zero
        carry[1, cols, :] = zero
        acc[cols, :] = zero
        return 0

    lax.fori_loop(0, nblk, stage, 0)

    def kstep(pp, _):
        p = nblk // SB_KB - 1 - pp
        key0 = p * span
        staged = pl.ds(pl.multiple_of(p * 2 * span, 2 * span), 2 * span)
        km = kms[:, staged]
        vm = vms[:, staged]

        def qtile(t, _):
            row0 = pl.multiple_of(t * qt, qt)
            rows = pl.ds(row0, qt)
            z_all = _dot(q_ref[0, rows, :], km)
            zs, sps, parts, masks = [], [], [], []
            for g in range(2 * SB_KB):
                causal = dist > key0 + (g // 2) * CHUNK - row0
                z = z_all[:, g * CHUNK:(g + 1) * CHUNK] + biases[g % 2]
                sp = _softplus(z)
                hi, lo = _split2(jnp.where(causal, sp, 0.0))
                zs.append(z)
                sps.append(sp)
                masks.append(causal)
                parts.append(jnp.concatenate([hi, lo], axis=1))
            r_all = _dot(jnp.concatenate(parts, axis=0), tt)
            a = [None] * (2 * SB_KB)
            for h in range(2):
                c = carry[h, rows, :]
                for blk in reversed(range(SB_KB)):
                    g = 2 * blk + h
                    r = r_all[g * qt:(g + 1) * qt]
                    e = jnp.exp(zs[g] - sps[g] - r[:, :CHUNK] - c)
                    a[g] = jnp.where(masks[g], e, 0.0).astype(BF16)
                    c = c + r[:, CHUNK:]
                carry[h, rows, :] = c
            acc[rows, :] = acc[rows, :] + _dot_nt(jnp.concatenate(a, axis=1), vm)
            return 0

        lax.fori_loop(key0 // qt, seq // qt, qtile, 0)
        return 0

    lax.fori_loop(0, nblk // SB_KB, kstep, 0)

    def write_out(i, _):
        rows = pl.ds(pl.multiple_of(i * CHUNK, CHUNK), CHUNK)
        o_ref[0, rows, :] = acc[rows, :].astype(BF16)
        return 0

    lax.fori_loop(0, nblk, write_out, 0)


def _sb_prompt(q, kt, vt, bias):
    b, seq, _ = q.shape
    qt = min(SB_QT, seq)
    blk = lambda: pl.BlockSpec((1, seq, LANES), lambda i, j: (i, 0, j))
    blk_t = lambda: pl.BlockSpec((1, LANES, seq), lambda i, j: (i, j, 0))
    return pl.pallas_call(
        functools.partial(_sb_prompt_kernel, qt),
        grid=(b, SB_W // LANES),
        in_specs=[pl.BlockSpec(memory_space=pltpu.SMEM), blk(), blk_t(), blk_t()],
        out_specs=blk(),
        out_shape=jax.ShapeDtypeStruct((b, seq, SB_W), BF16),
        scratch_shapes=[pltpu.VMEM((LANES, 2 * seq), BF16), pltpu.VMEM((LANES, 2 * seq), BF16),
                        pltpu.VMEM((2, seq, LANES), F32), pltpu.VMEM((seq, LANES), F32)],
        compiler_params=_cparams("parallel", "parallel"),
        name="sb_prompt",
    )(bias, q, kt, vt)


def _sb_sample_kernel(n_pages, pt_ref, q_ref, bias_ref, *refs):
    del pt_ref
    k_refs = refs[:n_pages]
    v_refs = refs[n_pages:2 * n_pages]
    o_ref = refs[2 * n_pages]
    head_of_lane = _iota((SB_HEADS, SB_W), 1) // SB_HEAD_DIM
    own = head_of_lane == _iota((SB_HEADS, SB_W), 0)
    qm = jnp.where(own, q_ref[0].astype(F32), 0.0).astype(BF16)
    pages = range(n_pages)
    kcat = jnp.concatenate([k_refs[p][0].astype(BF16) for p in pages], axis=1)
    z = _dot(qm, kcat)
    zr = jnp.concatenate([z[:, p * PAGE_SIZE:(p + 1) * PAGE_SIZE] + bias_ref[...] for p in pages], axis=0)
    sp = _softplus(zr)
    hi, lo = _split2(sp)
    r = _dot(jnp.concatenate([hi, lo], axis=1), _cumsum_matrix())
    carries = [None] * n_pages
    c = jnp.zeros((SB_HEADS, LANES), F32)
    for p in reversed(pages):
        carries[p] = c
        c = c + r[p * SB_HEADS:(p + 1) * SB_HEADS, PAGE_SIZE:]
    a = jnp.exp(zr - sp - r[:, :PAGE_SIZE] - jnp.concatenate(carries, axis=0))
    acat = jnp.concatenate([a[p * SB_HEADS:(p + 1) * SB_HEADS] for p in pages], axis=1).astype(BF16)
    vcat = jnp.concatenate([v_refs[p][0].astype(BF16) for p in pages], axis=1)
    o = _dot_nt(acat, vcat)
    o_ref[0] = jnp.sum(jnp.where(own, o, 0.0), axis=0, keepdims=True).astype(BF16)


def _sb_sample(q, cache_k, cache_v, page_table, bias):
    db, n_pages = page_table.shape
    n_pool = cache_k.shape[0]
    ck = cache_k.transpose(0, 2, 3, 1).reshape(n_pool, SB_W, PAGE_SIZE)
    cv = cache_v.transpose(0, 2, 3, 1).reshape(n_pool, SB_W, PAGE_SIZE)
    bias2d = jnp.broadcast_to(bias.astype(F32)[:, None], (SB_HEADS, LANES))

    def page_spec(p):
        return pl.BlockSpec((1, SB_W, PAGE_SIZE), lambda i, pt: (pt[i, p], 0, 0))

    grid_spec = pltpu.PrefetchScalarGridSpec(
        num_scalar_prefetch=1,
        grid=(db,),
        in_specs=[pl.BlockSpec((1, 1, SB_W), lambda i, pt: (i, 0, 0)),
                  pl.BlockSpec((SB_HEADS, LANES), lambda i, pt: (0, 0))]
        + [page_spec(p) for p in range(n_pages)] * 2,
        out_specs=pl.BlockSpec((1, 1, SB_W), lambda i, pt: (i, 0, 0)),
    )
    out = pl.pallas_call(
        functools.partial(_sb_sample_kernel, n_pages),
        grid_spec=grid_spec,
        out_shape=jax.ShapeDtypeStruct((db, 1, SB_W), BF16),
        compiler_params=_cparams("parallel"),
        name="sb_sample",
    )(page_table, q.reshape(db, 1, SB_W), bias2d, *([ck] * n_pages), *([cv] * n_pages))
    return out.reshape(db, SB_W)


def _head_ones():
    r = _iota((LANES, LANES), 0) // GDN_K_DIM
    c = _iota((LANES, LANES), 1) // GDN_K_DIM
    return jnp.where(r == c, 1.0, 0.0).astype(BF16)


def _unit_lower_solve(a, rhs):
    p = -a
    y = rhs
    for _ in range(6):
        r = _dot_ff(p, jnp.concatenate([p, y], axis=1))
        y = y + r[:, CHUNK:]
        p = r[:, :CHUNK]
    return y + _dot_ff(p, y)


GDN_TILE = 512


def _gdn_prompt_parts(tile, hp, qp_ref, kp_ref, vp_ref, z_ref, ba_ref, cwq_ref, cwk_ref, cwv_ref, alog_ref, dtb_ref,
                      nw_ref, o_ref, s_ref, qs, ks, vs, qn, kn, vn, exs, os_):
    seq = qp_ref.shape[1]
    pad = 8
    row = _iota((CHUNK, CHUNK), 0)
    col = _iota((CHUNK, CHUNK), 1)
    lower = row >= col
    strict = row > col
    first = col < GDN_K_DIM
    same_head = (row // GDN_K_DIM) == (col // GDN_K_DIM)
    head_ones = _head_ones()
    lower_ones = jnp.where(lower, 1.0, 0.0).astype(BF16)
    er = _iota((LANES, 4 * LANES), 0)
    ec = _iota((LANES, 4 * LANES), 1)
    sec = ec // LANES
    pair_head = 2 * hp + (ec % LANES) // GDN_K_DIM
    want = jnp.where(sec == 0, pair_head,
                     jnp.where(sec == 1, GDN_HEADS + pair_head,
                               jnp.where(sec == 2, GDN_HEADS + 2 * hp, GDN_HEADS + 2 * hp + 1)))
    expand = jnp.where(er == want, 1.0, 0.0).astype(BF16)
    lane_row = _iota((1, LANES), 1)
    neg_a = -jnp.exp(alog_ref[...])
    scale = GDN_K_DIM ** -0.5

    n_sub = tile // CHUNK

    def conv(scr, cw_ref, r0):
        w = scr[pl.ds(r0, tile + pad), :]
        y = (w[5:5 + tile] * cw_ref[0:1, :] + w[6:6 + tile] * cw_ref[1:2, :]
             + w[7:7 + tile] * cw_ref[2:3, :] + w[8:8 + tile] * cw_ref[3:4, :])
        return y * _sigmoid(y)

    def prepare(tt):
        r0 = pl.multiple_of(tt * tile, tile)
        rows = pl.ds(r0, tile)
        q = conv(qs, cwq_ref, r0)
        k = conv(ks, cwk_ref, r0)
        v = conv(vs, cwv_ref, r0)
        ss = _dot_xc(jnp.concatenate([q * q, k * k], axis=0), head_ones)
        qn[rows, :] = q * lax.rsqrt(ss[:tile] + L2_EPS) * scale
        kn[rows, :] = k * lax.rsqrt(ss[tile:] + L2_EPS)
        vn[rows, :] = v
        raw = ba_ref[0, rows, :]
        beta_raw = _sigmoid(raw)
        g_raw = neg_a * _softplus(raw + dtb_ref[...])
        g_wide = jnp.concatenate([g_raw[i * CHUNK:(i + 1) * CHUNK] for i in range(n_sub)], axis=1)
        c_wide = _dot_cx(lower_ones, g_wide)
        gcum = jnp.concatenate([c_wide[:, i * LANES:(i + 1) * LANES] for i in range(n_sub)], axis=0)
        exs[rows, :] = _dot_xc(jnp.where(lane_row < GDN_HEADS, beta_raw, gcum), expand)

    def setup():
        for scr, src in ((qs, qp_ref), (ks, kp_ref), (vs, vp_ref)):
            scr[0:pad, :] = jnp.zeros((pad, LANES), F32)
            scr[pad:seq + pad, :] = src[0]

        def body(tt, carry):
            prepare(tt)
            return carry

        lax.fori_loop(0, seq // tile, body, 0)

    def within_chunk(c):
        rows = pl.ds(pl.multiple_of(c * CHUNK, CHUNK), CHUNK)
        q = qn[rows, :]
        k = kn[rows, :]
        v = vn[rows, :]
        ex = exs[rows, :]
        beta_p = ex[:, 0:LANES]
        gc_p = ex[:, LANES:2 * LANES]
        eg_p = jnp.exp(gc_p)
        g_last = gc_p[CHUNK - 1:CHUNK, :]
        kb = k * beta_p
        lhs = jnp.concatenate([jnp.where(first, kb, 0.0), jnp.where(first, q, 0.0),
                               jnp.where(first, 0.0, kb), jnp.where(first, 0.0, q)], axis=0).astype(BF16)
        kq = _dot_nt(lhs, k.astype(BF16))
        rhs = jnp.concatenate([v * beta_p, kb * eg_p], axis=1)
        sols = []
        qks = []
        for h in range(2):
            g1 = ex[:, (2 + h) * LANES:(3 + h) * LANES]
            diff = g1 - g1.T
            dec = jnp.where(lower, jnp.exp(jnp.where(lower, diff, 0.0)), 0.0)
            kk = kq[2 * h * CHUNK:(2 * h + 1) * CHUNK]
            sols.append(_unit_lower_solve(jnp.where(strict, kk * dec, 0.0), rhs))
            qks.append((kq[(2 * h + 1) * CHUNK:(2 * h + 2) * CHUNK] * dec).astype(BF16))
        first2 = jnp.concatenate([first, first], axis=1)
        sol = jnp.where(first2, sols[0], sols[1])
        u = sol[:, 0:LANES]
        wq = jnp.concatenate([sol[:, LANES:2 * LANES], q * eg_p], axis=0).astype(BF16)
        k_tail_t = (k * jnp.exp(g_last - gc_p)).T.astype(BF16)
        return rows, u, wq, jnp.concatenate(qks, axis=1), k_tail_t, jnp.exp(g_last)

    def advance(s, parts):
        rows, u, wq, qk2, k_tail_t, decay = parts
        ws = _dot(wq, s.astype(BF16))
        v_new = u - ws[:CHUNK]
        v_heads = jnp.concatenate([jnp.where(first, v_new, 0.0), jnp.where(first, 0.0, v_new)], axis=0)
        os_[rows, :] = ws[CHUNK:] + _dot(qk2, v_heads.astype(BF16))
        upd = _dot(k_tail_t, v_new.astype(BF16))
        return s * decay + jnp.where(same_head, upd, 0.0)

    def normalize(tt, _):
        rows = pl.ds(pl.multiple_of(tt * tile, tile), tile)
        o = os_[rows, :]
        ms = _dot_xc(o * o, head_ones) * (1.0 / GDN_V_DIM)
        zf = z_ref[0, rows, :].astype(F32)
        o_ref[0, rows, :] = (o * lax.rsqrt(ms + RMS_EPS) * nw_ref[...] * (zf * _sigmoid(zf))).astype(BF16)
        return 0

    def finish(s):
        s_ref[0, 0] = s[0:GDN_K_DIM, 0:GDN_V_DIM]
        s_ref[0, 1] = s[GDN_K_DIM:, GDN_V_DIM:]
        lax.fori_loop(0, seq // tile, normalize, 0)

    return setup, within_chunk, advance, finish


GDN_GROUP = 4


def _gdn_prompt_kernel(tile, group, *refs):
    setup, within_chunk, advance, finish = _gdn_prompt_parts(tile, pl.program_id(1), *refs)
    seq = refs[0].shape[1]
    setup()

    def chunk_group(cg, s):
        parts = [within_chunk(cg * group + i) for i in range(group)]
        for p in parts:
            s = advance(s, p)
        return s

    finish(lax.fori_loop(0, seq // (CHUNK * group), chunk_group, jnp.zeros((LANES, LANES), F32)))


def _gdn_prompt(gqkv, z, ba, conv_w, alog_lane, dtb_lane, nw_lane):
    b, seq, _ = gqkv.shape
    n_pair = GDN_HEADS // 2
    blk = lambda off: pl.BlockSpec((1, seq, LANES), lambda i, j: (i, 0, off + j))
    cw = lambda off: pl.BlockSpec((GDN_CONV, LANES), lambda i, j: (0, off + j))
    lane_vec = lambda: pl.BlockSpec((1, LANES), lambda i, j: (0, 0))
    return pl.pallas_call(
        functools.partial(_gdn_prompt_kernel, min(GDN_TILE, seq), min(GDN_GROUP, seq // CHUNK)),
        grid=(b, n_pair),
        in_specs=[blk(0), blk(n_pair), blk(2 * n_pair), blk(0),
                  pl.BlockSpec((1, seq, LANES), lambda i, j: (i, 0, 0)),
                  cw(0), cw(n_pair), cw(2 * n_pair), lane_vec(), lane_vec(), lane_vec()],
        out_specs=[blk(0), pl.BlockSpec((1, 2, GDN_K_DIM, GDN_V_DIM), lambda i, j: (i, j, 0, 0))],
        out_shape=[jax.ShapeDtypeStruct((b, seq, GDN_V_W), BF16),
                   jax.ShapeDtypeStruct((b, GDN_HEADS, GDN_K_DIM, GDN_V_DIM), F32)],
        scratch_shapes=[pltpu.VMEM((seq + 8, LANES), F32)] * 3 + [pltpu.VMEM((seq, LANES), F32)] * 3
        + [pltpu.VMEM((seq, 4 * LANES), F32), pltpu.VMEM((seq, LANES), F32)],
        compiler_params=_cparams("parallel", "parallel"),
        name="gdn_prompt",
    )(gqkv, gqkv, gqkv, z, ba, conv_w, conv_w, conv_w, alog_lane, dtb_lane, nw_lane)


def _gdn_sample_kernel(alog_ref, dtb_ref, gq_ref, gk_ref, gv_ref, cq_ref, ck_ref, cv_ref, wq_ref, wk_ref, wv_ref,
                       ba_ref, z_ref, nw_ref, s_ref, o_ref, sn_ref):
    h = pl.program_id(0)

    def conv(g_ref, c_ref, w_ref):
        y = c_ref[0] * w_ref[0] + c_ref[1] * w_ref[1] + c_ref[2] * w_ref[2] + g_ref[...] * w_ref[3]
        return y * _sigmoid(y)

    q = conv(gq_ref, cq_ref, wq_ref)
    k = conv(gk_ref, ck_ref, wk_ref)
    v = conv(gv_ref, cv_ref, wv_ref)
    q = q * lax.rsqrt(jnp.sum(q * q, axis=0, keepdims=True) + L2_EPS) * (GDN_K_DIM ** -0.5)
    k = k * lax.rsqrt(jnp.sum(k * k, axis=0, keepdims=True) + L2_EPS)
    beta = _sigmoid(ba_ref[pl.ds(h, 1), :])
    decay = jnp.exp(-jnp.exp(alog_ref[h]) * _softplus(ba_ref[pl.ds(GDN_HEADS + h, 1), :] + dtb_ref[h]))
    w = k * (beta * decay)
    qd = q * decay
    ws = jnp.zeros_like(v)
    qs = jnp.zeros_like(v)
    for d in range(GDN_K_DIM):
        sd = s_ref[0, d]
        ws = ws + w[d:d + 1, :] * sd
        qs = qs + qd[d:d + 1, :] * sd
    v_new = v * beta - ws
    o = qs + jnp.sum(q * k, axis=0, keepdims=True) * v_new
    for d in range(GDN_K_DIM):
        sn_ref[0, d] = s_ref[0, d] * decay + k[d:d + 1, :] * v_new
    ms = jnp.mean(o * o, axis=0, keepdims=True)
    zf = z_ref[...].astype(F32)
    o_ref[...] = (o * lax.rsqrt(ms + RMS_EPS) * nw_ref[...] * (zf * _sigmoid(zf))).astype(BF16)


def _gdn_sample(gqkv, conv0, ba, z, state, conv_w, alog, dtb, norm_w):
    db = gqkv.shape[0]
    hd = GDN_K_DIM
    gt = gqkv.T
    c0t = conv0.transpose(1, 2, 0)
    cwt = conv_w.reshape(GDN_CONV, GDN_CONV_W, 1)
    bat = ba[:, :2 * GDN_HEADS].T
    st = state.transpose(1, 2, 3, 0)
    smem = pl.BlockSpec(memory_space=pltpu.SMEM)
    rows = lambda off: pl.BlockSpec((hd, db), lambda h: (off + h, 0))
    taps = lambda off: pl.BlockSpec((GDN_CONV - 1, hd, db), lambda h: (0, off + h, 0))
    wts = lambda off: pl.BlockSpec((GDN_CONV, hd, 1), lambda h: (0, off + h, 0))
    state_blk = pl.BlockSpec((1, hd, GDN_V_DIM, db), lambda h: (h, 0, 0, 0))
    nh = GDN_HEADS
    o_t, s_t = pl.pallas_call(
        _gdn_sample_kernel,
        grid=(nh,),
        in_specs=[smem, smem, rows(0), rows(nh), rows(2 * nh), taps(0), taps(nh), taps(2 * nh),
                  wts(0), wts(nh), wts(2 * nh),
                  pl.BlockSpec((2 * nh, db), lambda h: (0, 0)), rows(0),
                  pl.BlockSpec((GDN_V_DIM, 1), lambda h: (0, 0)), state_blk],
        out_specs=[rows(0), state_blk],
        out_shape=[jax.ShapeDtypeStruct((GDN_V_W, db), BF16), jax.ShapeDtypeStruct(st.shape, F32)],
        compiler_params=_cparams("parallel"),
        name="gdn_sample",
    )(alog.astype(F32), dtb.astype(F32), gt, gt, gt, c0t, c0t, c0t, cwt, cwt, cwt, bat, z.T,
      norm_w.astype(F32).reshape(GDN_V_DIM, 1), st)
    return o_t.T, s_t.transpose(3, 0, 1, 2)


def _mem_prompt_kernel(q_ref, k_ref, v_ref, o_ref):
    scale = MEM_HEAD_DIM ** -0.5
    for h in range(MEM_HEADS):
        lanes = slice(h * MEM_HEAD_DIM, (h + 1) * MEM_HEAD_DIM)
        s = _dot_nt(q_ref[0, :, lanes], k_ref[0, :, lanes].astype(BF16)) * scale
        m = jnp.max(s, axis=-1, keepdims=True)
        e = jnp.exp(s - m)
        p = e / jnp.sum(e, axis=-1, keepdims=True)
        o_ref[0, :, lanes] = _dot(p.astype(BF16), v_ref[0, :, lanes].astype(BF16)).astype(BF16)


def _mem_prompt(q, mem_k, mem_v, tq):
    b, seq, _ = q.shape
    return pl.pallas_call(
        _mem_prompt_kernel,
        grid=(b, seq // tq),
        in_specs=[pl.BlockSpec((1, tq, MEM_W), lambda i, j: (i, j, 0)),
                  pl.BlockSpec((1, N_MEM, MEM_W), lambda i, j: (i, 0, 0)),
                  pl.BlockSpec((1, N_MEM, MEM_W), lambda i, j: (i, 0, 0))],
        out_specs=pl.BlockSpec((1, tq, MEM_W), lambda i, j: (i, j, 0)),
        out_shape=jax.ShapeDtypeStruct((b, seq, MEM_W), BF16),
        compiler_params=_cparams("parallel", "parallel"),
        name="mem_prompt",
    )(q, mem_k, mem_v)


MEM_BT = 4


def _mem_sample_kernel(q_ref, k_ref, v_ref, o_ref):
    scale = MEM_HEAD_DIM ** -0.5
    rows = 8
    n_rows = N_MEM * MEM_HEADS
    own = _iota((rows, n_rows), 1) % MEM_HEADS == _iota((rows, n_rows), 0) % MEM_HEADS
    heads = range(MEM_HEADS)
    pad = jnp.zeros((rows - MEM_HEADS, MEM_HEAD_DIM), F32)
    for b in range(MEM_BT):
        qb = q_ref[b].astype(F32)
        q4 = jnp.concatenate([qb[:, h * MEM_HEAD_DIM:(h + 1) * MEM_HEAD_DIM] for h in heads] + [pad], axis=0)
        s = jnp.where(own, _dot_nt(q4.astype(BF16), k_ref[b].astype(BF16)) * scale, -jnp.inf)
        m = jnp.max(s, axis=-1, keepdims=True)
        e = jnp.exp(s - m)
        p = e / jnp.sum(e, axis=-1, keepdims=True)
        o = _dot(p.astype(BF16), v_ref[b].astype(BF16))
        o_ref[b] = jnp.concatenate([o[h:h + 1, :] for h in heads], axis=1).astype(BF16)


def _mem_sample(q, mem_k, mem_v):
    db = q.shape[0]
    out = pl.pallas_call(
        _mem_sample_kernel,
        grid=(db // MEM_BT,),
        in_specs=[pl.BlockSpec((MEM_BT, 1, MEM_W), lambda i: (i, 0, 0)),
                  pl.BlockSpec((MEM_BT, N_MEM * MEM_HEADS, MEM_HEAD_DIM), lambda i: (i, 0, 0)),
                  pl.BlockSpec((MEM_BT, N_MEM * MEM_HEADS, MEM_HEAD_DIM), lambda i: (i, 0, 0))],
        out_specs=pl.BlockSpec((MEM_BT, 1, MEM_W), lambda i: (i, 0, 0)),
        out_shape=jax.ShapeDtypeStruct((db, 1, MEM_W), BF16),
        compiler_params=_cparams("parallel"),
        name="mem_sample",
    )(q.reshape(db, 1, MEM_W), mem_k.reshape(db, N_MEM * MEM_HEADS, MEM_HEAD_DIM),
      mem_v.reshape(db, N_MEM * MEM_HEADS, MEM_HEAD_DIM))
    return out.reshape(db, MEM_W)


def _merge_kernel(x_ref, osb_ref, ogdn_ref, omem_ref, wg_ref, wsb_ref, wgdn_ref, wmem_ref, wo_ref, g_ref, b_ref,
                  h_ref):
    x = x_ref[...]
    xb = x.astype(BF16)
    merged = None
    for k, (o_ref, w_ref) in enumerate(((osb_ref, wsb_ref), (ogdn_ref, wgdn_ref), (omem_ref, wmem_ref))):
        gate = _sigmoid(_dot(xb, wg_ref[:, k * D_MODEL:(k + 1) * D_MODEL]))
        term = gate * _dot(o_ref[...], w_ref[...])
        merged = term if merged is None else merged + term
    pre = DN_ALPHA * x + _dot(merged.astype(BF16), wo_ref[...])
    h_ref[...] = _layer_norm(pre, g_ref[...], b_ref[...])


def _merge_ln(x2d, o_sb, o_gdn, o_mem, w_gates, w_sb, w_gdn, w_mem, w_o, g, b, tm):
    t = x2d.shape[0]
    row = lambda w: pl.BlockSpec((tm, w), lambda i: (i, 0))
    return pl.pallas_call(
        _merge_kernel,
        grid=(t // tm,),
        in_specs=[row(D_MODEL), row(512), row(512), row(512),
                  _const_spec((D_MODEL, 3 * D_MODEL)), _const_spec((512, D_MODEL)), _const_spec((512, D_MODEL)),
                  _const_spec((512, D_MODEL)), _const_spec((D_MODEL, D_MODEL)),
                  _const_spec((1, D_MODEL)), _const_spec((1, D_MODEL))],
        out_specs=row(D_MODEL),
        out_shape=jax.ShapeDtypeStruct((t, D_MODEL), F32),
        compiler_params=_cparams("parallel"),
        name="merge_ln",
    )(x2d, o_sb, o_gdn, o_mem, w_gates, w_sb, w_gdn, w_mem, w_o, g, b)


ROUTE_TM = 128
GROUP_SIZE = N_EXPERTS // N_GROUPS


def _first_max(x, idx, n):
    m = jnp.max(x, axis=0, keepdims=True)
    first = jnp.min(jnp.where(x == m, idx, n), axis=0, keepdims=True)
    return m, first


U32 = jnp.uint32
HALF = D_MODEL // 2


def _pack_rows(x):
    bits = pltpu.bitcast(x.astype(BF16).astype(F32), U32)
    return bits[:, :HALF] | (bits[:, HALF:] >> 16)


def _unpack_rows(w):
    return pltpu.bitcast(w & U32(0xFFFF0000), F32), pltpu.bitcast(w << 16, F32)


def _two_group_specs(tm, n_first, width):
    return [pl.BlockSpec((tm, width), lambda i: (jnp.minimum(i, n_first - 1), 0)),
            pl.BlockSpec((tm, width), lambda i: (jnp.maximum(i - n_first, 0), 0))]


def _route_kernel(n_first, hp_ref, hs_ref, wr_ref, bias_ref, e_ref, rank_ref, w_ref, cnt_ref, pk_ref, carry):
    i = pl.program_id(0)

    @pl.when(i == 0)
    def _():
        carry[...] = jnp.zeros_like(carry)

    tm = hp_ref.shape[0]
    neg = -jnp.inf
    h = jnp.where(i < n_first, hp_ref[...], hs_ref[...])
    pk_ref[...] = _pack_rows(h)
    hh, hl = _split2(h)
    wh, wl = _split2(wr_ref[...])
    logits = _dot_nt(wh, hh) + _dot_nt(wh, hl) + _dot_nt(wl, hh)
    scores = _sigmoid(logits)
    biased = scores + bias_ref[...]
    grp = []
    sub = _iota((GROUP_SIZE, tm), 0)
    for g in range(N_GROUPS):
        xg = biased[g * GROUP_SIZE:(g + 1) * GROUP_SIZE]
        m1, f1 = _first_max(xg, sub, GROUP_SIZE)
        m2 = jnp.max(jnp.where(sub == f1, neg, xg), axis=0, keepdims=True)
        grp.append(m1 + m2)
    gs = jnp.concatenate(grp, axis=0)
    gidx = _iota((N_GROUPS, tm), 0)
    gsel = gidx < 0
    for _ in range(TOPK_GROUPS):
        _, f = _first_max(gs, gidx, N_GROUPS)
        hit = gidx == f
        gsel = jnp.logical_or(gsel, hit)
        gs = jnp.where(hit, neg, gs)
    eidx = _iota((N_EXPERTS, tm), 0)
    gsel_f = jnp.where(gsel, 1.0, 0.0)
    allowed = jnp.concatenate(
        [jnp.broadcast_to(gsel_f[g:g + 1], (GROUP_SIZE, tm)) for g in range(N_GROUPS)], axis=0) > 0.5
    choice = jnp.where(allowed, biased, neg)
    picks = []
    weights = []
    chosen = jnp.zeros((N_EXPERTS, tm), F32)
    for _ in range(TOP_K):
        _, f = _first_max(choice, eidx, N_EXPERTS)
        hit = eidx == f
        picks.append(f)
        weights.append(jnp.sum(jnp.where(hit, scores, 0.0), axis=0, keepdims=True))
        chosen = jnp.where(hit, 1.0, chosen)
        choice = jnp.where(hit, neg, choice)
    wsum = weights[0]
    for w in weights[1:]:
        wsum = wsum + w
    tr = _iota((tm, tm), 0)
    tc = _iota((tm, tm), 1)
    before = jnp.where(tr < tc, 1.0, 0.0).astype(BF16)
    chosen_b = chosen.astype(BF16)
    rank = _dot(chosen_b, before) + carry[...]
    ranks = [jnp.sum(jnp.where(eidx == f, rank, 0.0), axis=0, keepdims=True) for f in picks]
    e_ref[...] = jnp.concatenate(picks, axis=0)
    rank_ref[...] = jnp.concatenate(ranks, axis=0).astype(I32)
    w_ref[...] = jnp.concatenate(weights, axis=0) / wsum * ROUTE_SCALE
    carry[...] = carry[...] + _dot(chosen_b, jnp.ones((tm, LANES), BF16))
    cnt_ref[...] = carry[...]


def _route(h_first, h_second, w_router_t, bias_col):
    tm = ROUTE_TM
    n_first = h_first.shape[0] // tm
    t = h_first.shape[0] + h_second.shape[0]
    col = lambda: pl.BlockSpec((TOP_K, tm), lambda i: (0, i))
    return pl.pallas_call(
        functools.partial(_route_kernel, n_first),
        grid=(t // tm,),
        in_specs=_two_group_specs(tm, n_first, D_MODEL)
        + [pl.BlockSpec((N_EXPERTS, D_MODEL), lambda i: (0, 0)),
           pl.BlockSpec((N_EXPERTS, 1), lambda i: (0, 0))],
        out_specs=[col(), col(), col(), pl.BlockSpec((N_EXPERTS, LANES), lambda i: (0, 0)),
                   pl.BlockSpec((tm, HALF), lambda i: (i, 0))],
        out_shape=[jax.ShapeDtypeStruct((TOP_K, t), I32), jax.ShapeDtypeStruct((TOP_K, t), I32),
                   jax.ShapeDtypeStruct((TOP_K, t), F32), jax.ShapeDtypeStruct((N_EXPERTS, LANES), F32),
                   jax.ShapeDtypeStruct((t, HALF), U32)],
        scratch_shapes=[pltpu.VMEM((N_EXPERTS, LANES), F32)],
        compiler_params=_cparams("arbitrary"),
        name="route",
    )(h_first, h_second, w_router_t, bias_col)


DISPATCH_TM = 128


def _row_copy(src, dst, s_row, d_row, sem):
    return pltpu.make_async_copy(src.at[pl.ds(s_row, 1)], dst.at[pl.ds(d_row, 1)], sem)


def _dispatch_kernel(dest_ref, h_ref, xs_ref, sem):
    def issue(t, _):
        for r in range(TOP_K):
            _row_copy(h_ref, xs_ref, t, dest_ref[t * TOP_K + r], sem).start(priority=r % 2)
        return 0

    lax.fori_loop(0, DISPATCH_TM, issue, 0)

    def drain(t, _):
        for r in range(TOP_K):
            _row_copy(h_ref, xs_ref, 0, 0, sem).wait()
        return 0

    lax.fori_loop(0, DISPATCH_TM, drain, 0)


def _dispatch(h_packed, dest_flat):
    t = h_packed.shape[0]
    return pl.pallas_call(
        _dispatch_kernel,
        grid=(t // DISPATCH_TM,),
        in_specs=[pl.BlockSpec((DISPATCH_TM * TOP_K,), lambda i: (i,), memory_space=pltpu.SMEM),
                  pl.BlockSpec((DISPATCH_TM, HALF), lambda i: (i, 0))],
        out_specs=pl.BlockSpec(memory_space=pl.ANY),
        out_shape=jax.ShapeDtypeStruct((t * TOP_K, HALF), U32),
        scratch_shapes=[pltpu.SemaphoreType.DMA(())],
        compiler_params=_cparams("arbitrary"),
        name="moe_dispatch",
    )(dest_flat, h_packed)


def _expert_kernel(blk_ref, exp_ref, lo_ref, hi_ref, first_ref, x_ref, wg_ref, wu_ref, wd_ref, y_ref, wgu, wdn):
    del blk_ref
    i = pl.program_id(0)

    @pl.when((i == 0) | (exp_ref[i] != exp_ref[jnp.maximum(i - 1, 0)]))
    def _():
        wgu[:, 0:D_EXPERT] = wg_ref[0].astype(BF16)
        wgu[:, D_EXPERT:] = wu_ref[0].astype(BF16)
        wdn[...] = wd_ref[0].astype(BF16)

    x = jnp.concatenate(_unpack_rows(x_ref[...]), axis=1).astype(BF16)
    gu = _dot(x, wgu[...])
    gate = gu[:, 0:D_EXPERT]
    act = gate * _sigmoid(gate) * gu[:, D_EXPERT:]
    y = _pack_rows(_dot(act.astype(BF16), wdn[...]))
    row = _iota((ROW_BLOCK, 1), 0)
    mine = (row >= lo_ref[i]) & (row < hi_ref[i])

    @pl.when(first_ref[i] == 1)
    def _():
        y_ref[...] = jnp.where(mine, y, U32(0))

    @pl.when(first_ref[i] == 0)
    def _():
        y_ref[...] = jnp.where(mine, y, y_ref[...])


def _expert_mm(xs, w_gate, w_up, w_down, blk, exp, lo, hi, first):
    n_items = blk.shape[0]
    grid_spec = pltpu.PrefetchScalarGridSpec(
        num_scalar_prefetch=5,
        grid=(n_items,),
        in_specs=[pl.BlockSpec((ROW_BLOCK, HALF), lambda i, blk, exp, lo, hi, fi: (blk[i], 0)),
                  pl.BlockSpec((1, D_MODEL, D_EXPERT), lambda i, blk, exp, lo, hi, fi: (exp[i], 0, 0)),
                  pl.BlockSpec((1, D_MODEL, D_EXPERT), lambda i, blk, exp, lo, hi, fi: (exp[i], 0, 0)),
                  pl.BlockSpec((1, D_EXPERT, D_MODEL), lambda i, blk, exp, lo, hi, fi: (exp[i], 0, 0))],
        out_specs=pl.BlockSpec((ROW_BLOCK, HALF), lambda i, blk, exp, lo, hi, fi: (blk[i], 0)),
        scratch_shapes=[pltpu.VMEM((D_MODEL, 2 * D_EXPERT), BF16), pltpu.VMEM((D_EXPERT, D_MODEL), BF16)],
    )
    return pl.pallas_call(
        _expert_kernel,
        grid_spec=grid_spec,
        out_shape=jax.ShapeDtypeStruct(xs.shape, U32),
        compiler_params=_cparams("arbitrary"),
        name="moe_experts",
    )(blk, exp, lo, hi, first, xs, w_gate, w_up, w_down)


COMBINE_TM = 128


def _combine_kernel(n_first, dest_ref, hp_ref, hs_ref, wt_ref, y_ref, wsg_ref, wsd_ref, g_ref, b_ref,
                    op_ref, os_ref, buf, sem):
    i = pl.program_id(0)

    def issue(t, _):
        for r in range(TOP_K):
            pltpu.make_async_copy(y_ref.at[pl.ds(dest_ref[t * TOP_K + r], 1)], buf.at[r, pl.ds(t, 1)],
                                  sem).start(priority=r % 2)
        return 0

    lax.fori_loop(0, COMBINE_TM, issue, 0)
    h = jnp.where(i < n_first, hp_ref[...], hs_ref[...])
    gu = _dot(h.astype(BF16), wsg_ref[...])
    gate = gu[:, 0:D_SHARED]
    f = _dot((gate * _sigmoid(gate) * gu[:, D_SHARED:]).astype(BF16), wsd_ref[...])

    def drain(t, _):
        for r in range(TOP_K):
            pltpu.make_async_copy(y_ref.at[pl.ds(0, 1)], buf.at[r, pl.ds(t, 1)], sem).wait()
        return 0

    lax.fori_loop(0, COMBINE_TM, drain, 0)
    lo = jnp.zeros((COMBINE_TM, HALF), F32)
    hi = jnp.zeros((COMBINE_TM, HALF), F32)
    for r in range(TOP_K):
        ya, yb = _unpack_rows(buf[r])
        w = wt_ref[:, r:r + 1]
        lo = lo + w * ya
        hi = hi + w * yb
    out = _layer_norm(DN_ALPHA * h + f + jnp.concatenate([lo, hi], axis=1), g_ref[...], b_ref[...])

    @pl.when(i < n_first)
    def _():
        op_ref[...] = out

    @pl.when(i >= n_first)
    def _():
        os_ref[...] = out


def _combine(h_first, h_second, w_tok, dest_flat, y, w_sgu, w_sd, g, b):
    tm = COMBINE_TM
    n_first = h_first.shape[0] // tm
    t = h_first.shape[0] + h_second.shape[0]
    return pl.pallas_call(
        functools.partial(_combine_kernel, n_first),
        grid=(t // tm,),
        in_specs=[pl.BlockSpec((tm * TOP_K,), lambda i: (i,), memory_space=pltpu.SMEM)]
        + _two_group_specs(tm, n_first, D_MODEL)
        + [pl.BlockSpec((tm, TOP_K), lambda i: (i, 0)),
           pl.BlockSpec(memory_space=pl.ANY),
           _const_spec((D_MODEL, 2 * D_SHARED)), _const_spec((D_SHARED, D_MODEL)),
           _const_spec((1, D_MODEL)), _const_spec((1, D_MODEL))],
        out_specs=_two_group_specs(tm, n_first, D_MODEL),
        out_shape=[jax.ShapeDtypeStruct(h_first.shape, F32), jax.ShapeDtypeStruct(h_second.shape, F32)],
        scratch_shapes=[pltpu.VMEM((TOP_K, tm, HALF), U32), pltpu.SemaphoreType.DMA(())],
        compiler_params=_cparams("arbitrary"),
        name="moe_combine",
    )(dest_flat, h_first, h_second, w_tok, y, w_sgu, w_sd, g, b)


def _moe_plan(counts, n_rows):
    n_blocks = n_rows // ROW_BLOCK
    n_items = n_blocks + N_EXPERTS
    start = jnp.cumsum(counts) - counts
    end = start + counts
    first_blk = start // ROW_BLOCK
    last_blk = jnp.maximum(end - 1, start) // ROW_BLOCK
    nb = jnp.where(counts > 0, last_blk - first_blk + 1, 0)
    item_end = jnp.cumsum(nb)
    item_start = item_end - nb
    total = item_end[-1]
    idx = jnp.arange(n_items, dtype=I32)
    last_item = jnp.maximum(total - 1, 0)
    src = jnp.minimum(idx, last_item)
    exp = jnp.minimum(jnp.sum(item_end[None, :] <= src[:, None], axis=1), N_EXPERTS - 1).astype(I32)
    onehot = exp[:, None] == jnp.arange(N_EXPERTS, dtype=I32)[None, :]
    take = lambda table: jnp.sum(jnp.where(onehot, table[None, :], 0), axis=1)
    blk = (take(first_blk) + src - take(item_start)).astype(I32)
    valid = idx < total
    lo = jnp.clip(take(start) - blk * ROW_BLOCK, 0, ROW_BLOCK)
    hi = jnp.clip(take(end) - blk * ROW_BLOCK, 0, ROW_BLOCK)
    first = (valid & (lo == 0)).astype(I32)
    lo = jnp.where(valid, lo, 0).astype(I32)
    hi = jnp.where(valid, hi, 0).astype(I32)
    return start, blk, exp, lo, hi, first


def _moe(h_first, h_second, w_router, router_bias, w_exp_gate, w_exp_up, w_exp_down, w_sgu, w_sd, g, b):
    t = h_first.shape[0] + h_second.shape[0]
    picks, ranks, weights, cnt, h_packed = _route(h_first, h_second, w_router.T,
                                                  router_bias.astype(F32).reshape(N_EXPERTS, 1))
    counts = cnt[:, 0].astype(I32)
    start, blk, exp, lo, hi, first = _moe_plan(counts, t * TOP_K)
    experts = jnp.arange(N_EXPERTS, dtype=I32)[:, None, None]
    dest = ranks + jnp.sum(jnp.where(picks[None] == experts, start.astype(I32)[:, None, None], 0), axis=0)
    dest_flat = dest.T.reshape(t * TOP_K)
    xs = _dispatch(h_packed, dest_flat)
    y = _expert_mm(xs, w_exp_gate, w_exp_up, w_exp_down, blk, exp, lo, hi, first)
    return _combine(h_first, h_second, weights.T, dest_flat, y, w_sgu, w_sd, g, b)


def kernel(x_prompt, x_sample, cache_sb_k, cache_sb_v, page_table, state_gdn, state_gdn_conv, cache_mem_k, cache_mem_v, mem_prompt, w_in, sb_logit_bias, conv_w, gdn_A_log, gdn_dt_bias, gdn_norm_w, w_mem_k, w_mem_v, w_br_sb, w_br_gdn, w_br_mem, w_o, ln1_g, ln1_b, w_router, router_bias, w_exp_gate, w_exp_up, w_exp_down, w_sh_gate, w_sh_up, w_sh_down, ln2_g, ln2_b):
    bsz, seq, _ = x_prompt.shape
    db = x_sample.shape[0]
    tp = bsz * seq

    ba_off = 3 * SB_W + GDN_CONV_W + GDN_V_W
    mq_off = ba_off + 2 * GDN_HEADS
    gate_off = mq_off + MEM_W
    w_main = jnp.concatenate([w_in[:, :SB_W], w_in[:, 3 * SB_W:ba_off], w_in[:, mq_off:gate_off]],
                             axis=1).astype(BF16)
    w_kvt = w_in[:, SB_W:3 * SB_W].T.astype(BF16)
    w_ba = jnp.pad(w_in[:, ba_off:mq_off], ((0, 0), (0, LANES - 2 * GDN_HEADS))).astype(BF16)
    w_gates = w_in[:, gate_off:].astype(BF16)
    w_kv = jnp.concatenate([w_mem_k, w_mem_v], axis=1).astype(BF16)
    w_sb, w_gdn, w_mem, w_ob = (w.astype(BF16) for w in (w_br_sb, w_br_gdn, w_br_mem, w_o))
    w_sgu = jnp.concatenate([w_sh_gate, w_sh_up], axis=1).astype(BF16)
    w_sd = w_sh_down.astype(BF16)
    row = lambda v: v.astype(F32).reshape(1, -1)
    head_lane = lambda v: jnp.pad(v.astype(F32), (GDN_HEADS, LANES - 2 * GDN_HEADS)).reshape(1, LANES)
    nw_pair = jnp.tile(gdn_norm_w.astype(F32), 2).reshape(1, LANES)
    bias = sb_logit_bias.astype(F32)

    xp = x_prompt.reshape(tp, D_MODEL)
    q, kt, vt, gqkv, z, mq, ba = _proj_in(xp, seq, w_main, w_kvt, w_ba, min(512, seq))
    mem_k, mem_v = _mem_kv(mem_prompt.reshape(bsz * N_MEM, D_MODEL), w_kv, 512)
    sh = lambda a: a.reshape(bsz, seq, a.shape[-1])
    o_sb = _sb_prompt(sh(q), kt, vt, bias)
    gq3 = sh(gqkv)
    o_gdn, s_p = _gdn_prompt(gq3, sh(z), sh(ba), conv_w.astype(F32), head_lane(gdn_A_log),
                             head_lane(gdn_dt_bias), nw_pair)
    o_mem = _mem_prompt(sh(mq), mem_k.reshape(bsz, N_MEM, MEM_W), mem_v.reshape(bsz, N_MEM, MEM_W), min(512, seq))
    h_p = _merge_ln(xp, o_sb.reshape(tp, SB_W), o_gdn.reshape(tp, GDN_V_W), o_mem.reshape(tp, MEM_W),
                    w_gates, w_sb, w_gdn, w_mem, w_ob, row(ln1_g), row(ln1_b), 512)

    xs = x_sample.reshape(db, D_MODEL)
    q_s, kt_s, vt_s, gqkv_s, z_s, mq_s, ba_s = _proj_in(xs, db, w_main, w_kvt, w_ba, db)
    o_sb_s = _sb_sample(q_s, cache_sb_k, cache_sb_v, page_table, bias)
    o_gdn_s, s_s = _gdn_sample(gqkv_s, state_gdn_conv.astype(F32), ba_s, z_s, state_gdn.astype(F32),
                               conv_w.astype(F32), gdn_A_log, gdn_dt_bias, gdn_norm_w)
    o_mem_s = _mem_sample(mq_s, cache_mem_k, cache_mem_v)
    h_s = _merge_ln(xs, o_sb_s, o_gdn_s, o_mem_s, w_gates, w_sb, w_gdn, w_mem, w_ob, row(ln1_g), row(ln1_b), db)

    y_p, y_s = _moe(h_p, h_s, w_router, router_bias, w_exp_gate, w_exp_up, w_exp_down,
                    w_sgu, w_sd, row(ln2_g), row(ln2_b))

    heads_p = lambda a: a.reshape(bsz, SB_HEADS, SB_HEAD_DIM, seq).transpose(0, 3, 1, 2)
    heads_s = lambda a: a[0].T.reshape(db, 1, SB_HEADS, SB_HEAD_DIM)
    return (y_p.reshape(bsz, seq, D_MODEL), y_s.reshape(db, 1, D_MODEL),
            heads_p(kt), heads_p(vt),
            s_p, gq3[:, seq - (GDN_CONV - 1):, :],
            mem_k.reshape(bsz, N_MEM, MEM_HEADS, MEM_HEAD_DIM), mem_v.reshape(bsz, N_MEM, MEM_HEADS, MEM_HEAD_DIM),
            heads_s(kt_s), heads_s(vt_s),
            s_s, jnp.concatenate([state_gdn_conv[:, 1:, :], gqkv_s[:, None, :]], axis=1))
```
